```python
import math
import jax
import jax.numpy as jnp
from jax import lax
import numpy as np

D_MODEL = 1024
BATCH = 2
SEQ = 8192
DEPTH = 2
DEC_BATCH = 8
DEC_SEQ = 8192
PAST_LEN = 128

HEAD_DIM = 64
N_HEADS_A = 4
DIFF_DIM = HEAD_DIM // 2
N_HEADS_B = 4
N_HEADS_C = 4
N_HEADS_D = 4
N_KV_D = 2
BRANCH_W = 4 * HEAD_DIM
N_BRANCH = 4
GRID_W = 64
NA_ROWS_MAX = 8
NA_COLS = 16
C_PATTERNS = ((128, 1), (512, 4), (2048, 16))
ROPE_THETA = 500000.0
ROPE_FRACTION = 4
AXIAL_THETA = 10000.0
Q_BLOCK = 128
D_FF = 2816
EPS = 1e-6
NEG_INF = -1e30
IN_SPLITS = (
    N_HEADS_A * HEAD_DIM, N_HEADS_A * HEAD_DIM, N_HEADS_A * HEAD_DIM,
    N_HEADS_B * HEAD_DIM, N_HEADS_B * HEAD_DIM, N_HEADS_B * HEAD_DIM,
    N_HEADS_C * HEAD_DIM, N_HEADS_C * HEAD_DIM, N_HEADS_C * HEAD_DIM,
    N_HEADS_D * HEAD_DIM, N_KV_D * HEAD_DIM, N_KV_D * HEAD_DIM,
    N_BRANCH * D_MODEL,
)
IN_COLS = 9 * BRANCH_W + N_HEADS_D * HEAD_DIM + 2 * N_KV_D * HEAD_DIM + N_BRANCH * D_MODEL

kernel_name = "hybrid_gated_encoder_4mixer"


def _rms_norm(x, g):
    xf = x.astype(jnp.float32)
    y = xf * lax.rsqrt(jnp.mean(xf * xf, axis=-1, keepdims=True) + EPS)
    return (y * g.astype(jnp.float32)).astype(x.dtype)


def _rope(x, pos, theta):
    half = x.shape[-1] // 2
    inv = jnp.exp(-math.log(theta) * jnp.arange(half, dtype=jnp.float32) / half)
    ang = pos.astype(jnp.float32)[:, None] * inv[None, :]
    ang = ang.reshape((pos.shape[0],) + (1,) * (x.ndim - 3) + (half,))
    cos, sin = jnp.cos(ang), jnp.sin(ang)
    xf = x.astype(jnp.float32)
    x1, x2 = xf[..., :half], xf[..., half:]
    return jnp.concatenate([x1 * cos - x2 * sin, x2 * cos + x1 * sin], axis=-1).astype(x.dtype)


def _partial_rope(x, pos):
    nr = x.shape[-1] // ROPE_FRACTION
    return jnp.concatenate([_rope(x[..., :nr], pos, ROPE_THETA), x[..., nr:]], axis=-1)


def _axial_rope(x, pos):
    half = x.shape[-1] // 2
    return jnp.concatenate([_rope(x[..., :half], pos // GRID_W, AXIAL_THETA),
                            _rope(x[..., half:], pos % GRID_W, AXIAL_THETA)], axis=-1)


def _sweep_query_blocks(q, fn):
    B, S = q.shape[:2]
    nb = S // Q_BLOCK
    qb = jnp.moveaxis(q.reshape((B, nb, Q_BLOCK) + q.shape[2:]), 1, 0)
    out = lax.map(fn, qb)
    return jnp.moveaxis(out, 0, 1).reshape((B, S) + out.shape[3:])


def _diff_attention(q, k, v, lam_vecs, subln_g, lam_init, pos):
    B, S, H = q.shape[:3]
    q = _partial_rope(q, pos)
    k = _partial_rope(k, pos)
    lv = lam_vecs.astype(jnp.float32)
    lam = jnp.exp(jnp.sum(lv[0] * lv[1])) - jnp.exp(jnp.sum(lv[2] * lv[3])) + lam_init
    scale = DIFF_DIM ** -0.5

    def block(qb):
        s = jnp.einsum("bqhcd,bkhcd->bhcqk", qb, k, preferred_element_type=jnp.float32) * scale
        p = jax.nn.softmax(s, axis=-1)
        a = p[:, :, 0] - lam * p[:, :, 1]
        return jnp.einsum("bhqk,bkhd->bqhd", a.astype(v.dtype), v)

    o = _sweep_query_blocks(q, block)
    o = _rms_norm(o, subln_g) * (1.0 - lam_init)
    return o.reshape(B, S, H * HEAD_DIM)


def _neighbourhood_attention(q, k, v, rpb):
    B, S, H, dh = q.shape
    rows = S // GRID_W
    kr = min(NA_ROWS_MAX, rows)
    r = jnp.arange(rows)
    row_idx = jnp.clip(r - kr // 2, 0, rows - kr)[:, None] + jnp.arange(kr)[None, :]
    c = jnp.arange(GRID_W)
    col_start = jnp.clip(c - NA_COLS // 2, 0, GRID_W - NA_COLS)
    col_in = (c[None, :] >= col_start[:, None]) & (c[None, :] < col_start[:, None] + NA_COLS)
    dr = row_idx - r[:, None] + NA_ROWS_MAX - 1
    dc = jnp.clip(c[None, :] - c[:, None] + NA_COLS - 1, 0, 2 * NA_COLS - 2)
    bias = rpb.astype(jnp.float32)[:, dr][:, :, :, dc]
    bias = jnp.where(col_in[None, None, None], bias, NEG_INF).transpose(1, 0, 3, 2, 4)
    qg = q.reshape(B, rows, GRID_W, H, dh)
    kg = jnp.take(k.reshape(B, rows, GRID_W, H, dh), row_idx, axis=1)
    vg = jnp.take(v.reshape(B, rows, GRID_W, H, dh), row_idx, axis=1)
    s = jnp.einsum("brchd,brkmhd->brhckm", qg, kg, preferred_element_type=jnp.float32) * dh ** -0.5 + bias
    p = jax.nn.softmax(s.reshape(B, rows, H, GRID_W, kr * GRID_W), axis=-1).reshape(s.shape)
    o = jnp.einsum("brhckm,brkmhd->brchd", p.astype(v.dtype), vg)
    return o.reshape(B, S, H * dh)


def _to_sub(x, d):
    B, S = x.shape[:2]
    r = x.reshape((B, S // d, d) + x.shape[2:])
    return jnp.moveaxis(r, 2, 1).reshape((B * d, S // d) + x.shape[2:])


def _from_sub(x, d, B):
    L = x.shape[1]
    r = x.reshape((B, d, L) + x.shape[2:])
    return jnp.moveaxis(r, 1, 2).reshape((B, L * d) + x.shape[2:])


def _banded_attention(q, k, v, half):
    N, L, H, dh = q.shape
    nb = -(-L // Q_BLOCK)
    lq = nb * Q_BLOCK
    kw = Q_BLOCK + 2 * half
    qb = jnp.pad(q, ((0, 0), (0, lq - L), (0, 0), (0, 0))).reshape(N, nb, Q_BLOCK, H, dh)
    pad = ((0, 0), (half, lq - L + half), (0, 0), (0, 0))
    idx = jnp.arange(nb)[:, None] * Q_BLOCK + jnp.arange(kw)[None, :]
    kb = jnp.take(jnp.pad(k, pad), idx, axis=1)
    vb = jnp.take(jnp.pad(v, pad), idx, axis=1)
    s = jnp.einsum("nbqhd,nbkhd->nbhqk", qb, kb, preferred_element_type=jnp.float32) * dh ** -0.5
    key_pos = idx - half
    q_pos = jnp.arange(nb)[:, None] * Q_BLOCK + jnp.arange(Q_BLOCK)[None, :]
    rel = key_pos[:, None, :] - q_pos[:, :, None]
    valid = (jnp.abs(rel) <= half) & (key_pos[:, None, :] >= 0) & (key_pos[:, None, :] < L)
    s = jnp.where(valid[None, :, None], s, NEG_INF)
    lse = jax.nn.logsumexp(s, axis=-1)
    p = jnp.exp(s - lse[..., None])
    o = jnp.einsum("nbhqk,nbkhd->nbqhd", p.astype(v.dtype), vb).reshape(N, lq, H, dh)[:, :L]
    lse = jnp.swapaxes(lse, 2, 3).reshape(N, lq, H)[:, :L]
    return o, lse


def _dilated_attention(q, k, v, pos):
    B, S, H, dh = q.shape
    q = _partial_rope(q, pos)
    k = _partial_rope(k, pos)
    outs, lses = [], []
    for window, dil in C_PATTERNS:
        half = window // (2 * dil)
        o, lse = _banded_attention(_to_sub(q, dil), _to_sub(k, dil), _to_sub(v, dil), half)
        outs.append(_from_sub(o, dil, B))
        lses.append(_from_sub(lse, dil, B))
    wgt = jax.nn.softmax(jnp.stack(lses, axis=0), axis=0)
    o = jnp.einsum("nbsh,nbshd->bshd", wgt, jnp.stack(outs, axis=0).astype(jnp.float32))
    return o.astype(v.dtype).reshape(B, S, H * dh)


def _axial_gqa(q, k, v, g_q, g_k, pos):
    B, S, H, dh = q.shape
    q = _axial_rope(_rms_norm(q, g_q), pos)
    k = _axial_rope(_rms_norm(k, g_k), pos)
    q = q.reshape(B, S, N_KV_D, H // N_KV_D, dh)
    scale = dh ** -0.5

    def block(qb):
        s = jnp.einsum("bqkgd,bskd->bkgqs", qb, k, preferred_element_type=jnp.float32) * scale
        p = jax.nn.softmax(s, axis=-1)
        return jnp.einsum("bkgqs,bskd->bqkgd", p.astype(v.dtype), v)

    o = _sweep_query_blocks(q, block)
    return o.reshape(B, S, H * dh)


def _layer(x, pos, lam_init, norm_attn, w_in, diff_lambda, diff_subln, na_rpb, qk_norm,
           w_branch, w_out, norm_mlp, w_up, conv_w, conv_b, w_down):
    B, S, _ = x.shape
    h = _rms_norm(x, norm_attn)
    proj = h @ w_in
    cuts = np.cumsum(IN_SPLITS)[:-1].tolist()
    aq, ak, av, bq, bk, bv, cq, ck, cv, dq, dk, dv, gate = jnp.split(proj, cuts, axis=-1)
    o_a = _diff_attention(aq.reshape(B, S, N_HEADS_A, 2, DIFF_DIM), ak.reshape(B, S, N_HEADS_A, 2, DIFF_DIM),
                          av.reshape(B, S, N_HEADS_A, HEAD_DIM), diff_lambda, diff_subln, lam_init, pos)
    o_b = _neighbourhood_attention(bq.reshape(B, S, N_HEADS_B, HEAD_DIM), bk.reshape(B, S, N_HEADS_B, HEAD_DIM),
                                   bv.reshape(B, S, N_HEADS_B, HEAD_DIM), na_rpb)
    o_c = _dilated_attention(cq.reshape(B, S, N_HEADS_C, HEAD_DIM), ck.reshape(B, S, N_HEADS_C, HEAD_DIM),
                             cv.reshape(B, S, N_HEADS_C, HEAD_DIM), pos)
    o_d = _axial_gqa(dq.reshape(B, S, N_HEADS_D, HEAD_DIM), dk.reshape(B, S, N_KV_D, HEAD_DIM),
                     dv.reshape(B, S, N_KV_D, HEAD_DIM), qk_norm[0], qk_norm[1], pos)
    gates = jax.nn.sigmoid(gate.reshape(B, S, N_BRANCH, D_MODEL))
    merged = gates[:, :, 0] * (o_a @ w_branch[0])
    merged = merged + gates[:, :, 1] * (o_b @ w_branch[1])
    merged = merged + gates[:, :, 2] * (o_c @ w_branch[2])
    merged = merged + gates[:, :, 3] * (o_d @ w_branch[3])
    x = x + merged @ w_out
    h = _rms_norm(x, norm_mlp)
    u = h @ w_up
    up = jnp.pad(u, ((0, 0), (1, 1), (0, 0)))
    u = up[:, :-2] * conv_w[0] + up[:, 1:-1] * conv_w[1] + up[:, 2:] * conv_w[2] + conv_b
    val, gt = jnp.split(u, 2, axis=-1)
    return x + (jax.nn.gelu(gt, approximate=False) * val) @ w_down


def _trunk(x, norm_attn, w_in, diff_lambda, diff_subln, na_rpb, qk_norm, w_branch, w_out,
           norm_mlp, w_up, conv_w, conv_b, w_down, norm_final):
    pos = jnp.arange(x.shape[1], dtype=jnp.int32)
    for l in range(DEPTH):
        lam_init = 0.8 - 0.6 * math.exp(-0.3 * l)
        x = _layer(x, pos, lam_init, norm_attn[l], w_in[l], diff_lambda[l], diff_subln[l], na_rpb[l],
                   qk_norm[l], w_branch[l], w_out[l], norm_mlp[l], w_up[l], conv_w[l], conv_b[l], w_down[l])
    return _rms_norm(x, norm_final)


def setup_inputs(seed: int = 0) -> dict:
    key = jax.random.key(seed)
    ks = jax.random.split(key, 16)
    f32 = jnp.float32
    nrm = lambda k, shape, s: jax.random.normal(k, shape, f32) * s
    return {
        "x_prompt": nrm(ks[0], (BATCH, SEQ, D_MODEL), 1.0),
        "x_sample": nrm(ks[1], (DEC_BATCH, DEC_SEQ, D_MODEL), 1.0),
        "norm_attn": 1.0 + nrm(ks[2], (DEPTH, D_MODEL), 0.02),
        "w_in": nrm(ks[3], (DEPTH, D_MODEL, IN_COLS), D_MODEL ** -0.5),
        "diff_lambda": nrm(ks[4], (DEPTH, 4, DIFF_DIM), 0.1),
        "diff_subln": 1.0 + nrm(ks[5], (DEPTH, HEAD_DIM), 0.02),
        "na_rpb": nrm(ks[6], (DEPTH, N_HEADS_B, 2 * NA_ROWS_MAX - 1, 2 * NA_COLS - 1), 0.1),
        "qk_norm": 1.0 + nrm(ks[7], (DEPTH, 2, HEAD_DIM), 0.02),
        "w_branch": nrm(ks[8], (DEPTH, N_BRANCH, BRANCH_W, D_MODEL), BRANCH_W ** -0.5),
        "w_out": nrm(ks[9], (DEPTH, D_MODEL, D_MODEL), D_MODEL ** -0.5),
        "norm_mlp": 1.0 + nrm(ks[10], (DEPTH, D_MODEL), 0.02),
        "w_up": nrm(ks[11], (DEPTH, D_MODEL, 2 * D_FF), D_MODEL ** -0.5),
        "conv_w": nrm(ks[12], (DEPTH, 3, 2 * D_FF), 3 ** -0.5),
        "conv_b": nrm(ks[13], (DEPTH, 2 * D_FF), 0.02),
        "w_down": nrm(ks[14], (DEPTH, D_FF, D_MODEL), D_FF ** -0.5),
        "norm_final": 1.0 + nrm(ks[15], (D_MODEL,), 0.02),
    }


def reference(x_prompt, x_sample, norm_attn, w_in, diff_lambda, diff_subln, na_rpb, qk_norm, w_branch,
              w_out, norm_mlp, w_up, conv_w, conv_b, w_down, norm_final):
    y_prompt = _trunk(x_prompt, norm_attn, w_in, diff_lambda, diff_subln, na_rpb, qk_norm, w_branch, w_out,
                      norm_mlp, w_up, conv_w, conv_b, w_down, norm_final)
    y_sample = _trunk(x_sample, norm_attn, w_in, diff_lambda, diff_subln, na_rpb, qk_norm, w_branch, w_out,
                      norm_mlp, w_up, conv_w, conv_b, w_down, norm_final)
    return (y_prompt, y_sample)
```

```python
import functools
import math

import jax
import jax.numpy as jnp
import numpy as np
from jax import lax
from jax.experimental import pallas as pl
from jax.experimental.pallas import tpu as pltpu

F32 = jnp.float32
BF16 = jnp.bfloat16

D_MODEL = 1024
HEAD_DIM = 64
DIFF_DIM = 32
BRANCH_W = 256
N_BRANCH = 4
GRID_W = 64
NA_ROWS = 8
NA_COLS = 16
C_PATTERNS = ((128, 1), (512, 4), (2048, 16))
ROPE_THETA = 500000.0
AXIAL_THETA = 10000.0
D_FF = 2816
EPS = 1e-6
NEG_INF = -1e30
DEPTH = 2
IN_COLS = 6912
N_COL_BLOCKS = IN_COLS // BRANCH_W
N_GATE_BLOCKS = N_BRANCH * D_MODEL // BRANCH_W
N_MIX_COLS = IN_COLS - N_BRANCH * D_MODEL
LOG2E = 1.4426950408889634

(CB_AQ, CB_AK, CB_AV, CB_BQ, CB_BK, CB_BV, CB_CQ, CB_CK, CB_CV, CB_DQ,
 CB_DKV) = range(N_GATE_BLOCKS, N_GATE_BLOCKS + 11)

VMEM_LIMIT = 56 * 1024 * 1024

TM_PROJ = 1024
TQ_FULL = 512
TK_FULL = 256
TQ_WIN = 128
C_HALF = 64
B_QROWS = 2
B_KROWS = 10
TM_MERGE = 512
TM_MLP = 1024
FF_CHUNK = 256


def _params(*sem):
    return pltpu.CompilerParams(dimension_semantics=sem, vmem_limit_bytes=VMEM_LIMIT)


def _rotate_pairs(y, cos, sin, half, group):
    lane = lax.broadcasted_iota(jnp.int32, y.shape, 1) % group
    second = (lane >= half) & (lane < 2 * half)
    from_below = pltpu.roll(y, half, 1)
    from_above = pltpu.roll(y, y.shape[1] - half, 1)
    return y * cos + jnp.where(second, from_below, from_above) * sin


def _head_rms(y, gmat, gain):
    sq = y * y
    hi = sq.astype(BF16)
    lo = (sq - hi.astype(F32)).astype(BF16)
    ms = (jnp.dot(hi, gmat, preferred_element_type=F32)
          + jnp.dot(lo, gmat, preferred_element_type=F32))
    return y * lax.rsqrt(ms + EPS) * gain


def _in_proj_kernel(x_ref, g_ref, w_ref, tab_ref, gmat_ref, qkg_ref, o_ref, h_ref):
    j = pl.program_id(2)

    @pl.when(j == 0)
    def _():
        x = x_ref[0]
        ms = jnp.mean(x * x, axis=-1, keepdims=True)
        h_ref[...] = ((x * lax.rsqrt(ms + EPS)) * g_ref[...]).astype(BF16)

    acc = jnp.dot(h_ref[...], w_ref[...], preferred_element_type=F32)
    scale_a = DIFF_DIM ** -0.5 * LOG2E
    scale_d = HEAD_DIM ** -0.5 * LOG2E
    scale_w = HEAD_DIM ** -0.5

    def rope_a(y):
        return _rotate_pairs(y, tab_ref[0], tab_ref[1], 4, 32)

    def rope_c(y):
        return _rotate_pairs(y, tab_ref[2], tab_ref[3], 8, 64)

    def rope_d(y):
        return _rotate_pairs(y, tab_ref[4], tab_ref[5], 16, 32)

    def put(y):
        o_ref[0] = y.astype(BF16)

    @pl.when(j == CB_AQ)
    def _():
        put(rope_a(acc) * scale_a)

    @pl.when(j == CB_AK)
    def _():
        put(rope_a(acc))

    @pl.when((j == CB_AV) | (j == CB_BK) | (j == CB_BV) | (j == CB_CV))
    def _():
        put(acc)

    @pl.when(j == CB_BQ)
    def _():
        put(acc * scale_w)

    @pl.when(j == CB_CQ)
    def _():
        put(rope_c(acc) * scale_w)

    @pl.when(j == CB_CK)
    def _():
        put(rope_c(acc))

    @pl.when(j == CB_DQ)
    def _():
        put(rope_d(_head_rms(acc, gmat_ref[...], qkg_ref[0:1, :])) * scale_d)

    @pl.when(j == CB_DKV)
    def _():
        lane = lax.broadcasted_iota(jnp.int32, acc.shape, 1)
        keys = rope_d(_head_rms(acc, gmat_ref[...], qkg_ref[1:2, :]))
        put(jnp.where(lane < 2 * HEAD_DIM, keys, acc))

    @pl.when(j < N_GATE_BLOCKS)
    def _():
        put(jax.nn.sigmoid(acc))


def _in_proj(x, gain, w_bf16, tables, gmat, qk_gain):
    B, S, _ = x.shape
    tm = min(TM_PROJ, S)
    grid = (S // tm, B, N_COL_BLOCKS)
    return pl.pallas_call(
        _in_proj_kernel,
        grid=grid,
        in_specs=[
            pl.BlockSpec((1, tm, D_MODEL), lambda i, b, j: (b, i, 0)),
            pl.BlockSpec((1, D_MODEL), lambda i, b, j: (0, 0)),
            pl.BlockSpec((D_MODEL, BRANCH_W), lambda i, b, j: (0, j)),
            pl.BlockSpec((6, tm, BRANCH_W), lambda i, b, j: (0, i, 0)),
            pl.BlockSpec((BRANCH_W, BRANCH_W), lambda i, b, j: (0, 0)),
            pl.BlockSpec((2, BRANCH_W), lambda i, b, j: (0, 0)),
        ],
        out_specs=pl.BlockSpec((1, tm, BRANCH_W), lambda i, b, j: (b, i, j)),
        out_shape=jax.ShapeDtypeStruct((B, S, IN_COLS), BF16),
        scratch_shapes=[pltpu.VMEM((tm, D_MODEL), BF16)],
        compiler_params=_params("arbitrary", "arbitrary", "arbitrary"),
        name="in_proj",
    )(x, gain, w_bf16, tables, gmat, qk_gain)


def _rope_tables(S):
    pos = jnp.arange(S, dtype=jnp.int32)
    lane = np.arange(BRANCH_W)

    def build(group, half, ang_of_lane):
        m = lane % group
        first = m < half
        second = (m >= half) & (m < 2 * half)
        ang = ang_of_lane
        cos = jnp.where(jnp.asarray(first | second)[None, :], jnp.cos(ang), 1.0)
        sin = jnp.where(jnp.asarray(second)[None, :], jnp.sin(ang),
                        jnp.where(jnp.asarray(first)[None, :], -jnp.sin(ang), 0.0))
        return cos.astype(F32), sin.astype(F32)

    def angles(p, theta, half, idx):
        inv = jnp.exp(-math.log(theta) * jnp.arange(half, dtype=F32) / half)
        ang = p.astype(F32)[:, None] * inv[None, :]
        return ang[:, idx]

    ca, sa = build(32, 4, angles(pos, ROPE_THETA, 4, (lane % 32) % 4))
    cc, sc = build(64, 8, angles(pos, ROPE_THETA, 8, (lane % 64) % 8))
    idx = (lane % 32) % 16
    ang_row = angles(pos // GRID_W, AXIAL_THETA, 16, idx)
    ang_col = angles(pos % GRID_W, AXIAL_THETA, 16, idx)
    ang_d = jnp.where(jnp.asarray((lane % 64) < 32)[None, :], ang_row, ang_col)
    cd, sd = build(32, 16, ang_d)
    return jnp.stack([ca, sa, cc, sc, cd, sd], axis=0)


def _full_attn_kernel(lam_ref, q_ref, k_ref, v_ref, sub_ref, o_ref, vt_ref, qpad_ref, ot_ref, *,
                      heads, v_row0, diff, post_scale):
    S = k_ref.shape[1]
    tq = q_ref.shape[1]
    tk = min(TK_FULL, S)
    n_kt = S // tk

    @pl.when(pl.program_id(1) == 0)
    def _():
        chunk = min(512, S)

        def tr(c, carry):
            r0 = pl.multiple_of(c * chunk, chunk)
            vc = v_ref[0, pl.ds(r0, chunk), :].astype(F32)
            vt_ref[:, pl.ds(r0, chunk)] = vc.T.astype(BF16)
            return carry

        lax.fori_loop(0, S // chunk, tr, 0)

    qt = q_ref[0].astype(F32).T.astype(BF16)
    for vh, (slo, shi, dlo, _) in enumerate(heads):
        qpad_ref[vh] = jnp.zeros((BRANCH_W, tq), BF16)
        qpad_ref[vh, dlo:dlo + (shi - slo), :] = qt[slo:shi, :]

    def run_head(vh, v_head):
        vlo = v_row0 + HEAD_DIM * v_head

        def body(kt, carry):
            m, l, acc = carry
            r0 = pl.multiple_of(kt * tk, tk)
            k = k_ref[0, pl.ds(r0, tk), :]
            s = jnp.dot(k, qpad_ref[vh], preferred_element_type=F32)
            m_new = jnp.maximum(m, jnp.max(s, axis=0, keepdims=True))
            alpha = jnp.exp2(m - m_new)
            p = jnp.exp2(s - m_new)
            l = alpha * l + jnp.sum(p, axis=0, keepdims=True)
            vt = vt_ref[vlo:vlo + HEAD_DIM, pl.ds(r0, tk)]
            acc = alpha * acc + jnp.dot(vt, p.astype(BF16), preferred_element_type=F32)
            return m_new, l, acc

        init = (jnp.full((1, tq), NEG_INF, F32), jnp.zeros((1, tq), F32),
                jnp.zeros((HEAD_DIM, tq), F32))
        _, l, acc = lax.fori_loop(0, n_kt, body, init)
        return acc / l

    if diff:
        lam = lam_ref[0]
        for h in range(len(heads) // 2):
            o = run_head(2 * h, heads[2 * h][3]) - lam * run_head(2 * h + 1, heads[2 * h + 1][3])
            ms = jnp.mean(o * o, axis=0, keepdims=True)
            y = (o * lax.rsqrt(ms + EPS)) * sub_ref[...]
            ot_ref[HEAD_DIM * h:HEAD_DIM * (h + 1), :] = y * post_scale
    else:
        for h in range(len(heads)):
            ot_ref[HEAD_DIM * h:HEAD_DIM * (h + 1), :] = run_head(h, heads[h][3])
    o_ref[0] = ot_ref[...].T.astype(BF16)


def _full_attention(proj, lam, sub_gain, *, q_cb, k_cb, v_cb, heads, v_row0, diff, post_scale):
    B, S, _ = proj.shape
    tq = min(TQ_FULL, S)
    kern = functools.partial(_full_attn_kernel, heads=heads, v_row0=v_row0, diff=diff,
                             post_scale=post_scale)
    return pl.pallas_call(
        kern,
        grid=(B, S // tq),
        in_specs=[
            pl.BlockSpec(memory_space=pltpu.SMEM),
            pl.BlockSpec((1, tq, BRANCH_W), lambda b, i: (b, i, q_cb)),
            pl.BlockSpec((1, S, BRANCH_W), lambda b, i: (b, 0, k_cb)),
            pl.BlockSpec((1, S, BRANCH_W), lambda b, i: (b, 0, v_cb)),
            pl.BlockSpec((HEAD_DIM, tq), lambda b, i: (0, 0)),
        ],
        out_specs=pl.BlockSpec((1, tq, BRANCH_W), lambda b, i: (b, i, 0)),
        out_shape=jax.ShapeDtypeStruct((B, S, BRANCH_W), BF16),
        scratch_shapes=[
            pltpu.VMEM((BRANCH_W, S), BF16),
            pltpu.VMEM((len(heads), BRANCH_W, tq), BF16),
            pltpu.VMEM((BRANCH_W, tq), F32),
        ],
        compiler_params=_params("arbitrary", "arbitrary"),
        name="full_attn_diff" if diff else "full_attn_gqa",
    )(lam, proj, proj, proj, sub_gain)


_HEADS_A = tuple((HEAD_DIM * h + DIFF_DIM * c, HEAD_DIM * h + DIFF_DIM * (c + 1),
                  HEAD_DIM * h + DIFF_DIM * c, h) for h in range(4) for c in range(2))
_HEADS_D = tuple((HEAD_DIM * h, HEAD_DIM * (h + 1), HEAD_DIM * (h // 2), h // 2) for h in range(4))


def _attend_heads(q, kw, vw, bias_of_head, want_lse):
    tq = q.shape[0]
    lane = lax.broadcasted_iota(jnp.int32, (tq, BRANCH_W), 1)
    o = jnp.zeros((tq, BRANCH_W), F32)
    lse = jnp.zeros((tq, BRANCH_W), F32)
    for h in range(4):
        in_head = (lane >= HEAD_DIM * h) & (lane < HEAD_DIM * (h + 1))
        qh = jnp.where(in_head, q, jnp.zeros_like(q))
        s = lax.dot_general(qh, kw, (((1,), (1,)), ((), ())), preferred_element_type=F32)
        s = bias_of_head(h, s)
        m = jnp.max(s, axis=-1, keepdims=True)
        p = jnp.exp(s - m)
        l = jnp.sum(p, axis=-1, keepdims=True)
        of = jnp.dot(p.astype(BF16), vw, preferred_element_type=F32)
        o = jnp.where(in_head, of / l, o)
        if want_lse:
            lse = jnp.where(in_head, m + jnp.log(l), lse)
    return o, lse


def _band_kernel(q_ref, k_ref, v_ref, o_ref, lse_ref):
    L = k_ref.shape[1]
    tq = q_ref.shape[1]
    kwin = tq + 2 * C_HALF
    q0 = pl.program_id(2) * tq
    ks = pl.multiple_of(jnp.clip(q0 - C_HALF, 0, L - kwin), C_HALF)
    kw = k_ref[0, pl.ds(ks, kwin), :]
    vw = v_ref[0, pl.ds(ks, kwin), :]
    qpos = q0 + lax.broadcasted_iota(jnp.int32, (tq, kwin), 0)
    kpos = ks + lax.broadcasted_iota(jnp.int32, (tq, kwin), 1)
    valid = jnp.abs(qpos - kpos) <= C_HALF

    def mask(h, s):
        return jnp.where(valid, s, NEG_INF)

    o, lse = _attend_heads(q_ref[0], kw, vw, mask, True)
    o_ref[0] = o.astype(BF16)
    lse_ref[0] = lse


def _dilated_pattern(proj, dil):
    B, S, _ = proj.shape
    L = S // dil
    tq = min(TQ_WIN, L)
    view = proj.reshape(B, L, dil * IN_COLS)
    o, lse = pl.pallas_call(
        _band_kernel,
        grid=(B, dil, L // tq),
        in_specs=[
            pl.BlockSpec((1, tq, BRANCH_W), lambda b, r, i: (b, i, r * N_COL_BLOCKS + CB_CQ)),
            pl.BlockSpec((1, L, BRANCH_W), lambda b, r, i: (b, 0, r * N_COL_BLOCKS + CB_CK)),
            pl.BlockSpec((1, L, BRANCH_W), lambda b, r, i: (b, 0, r * N_COL_BLOCKS + CB_CV)),
        ],
        out_specs=[
            pl.BlockSpec((1, tq, BRANCH_W), lambda b, r, i: (b, i, r)),
            pl.BlockSpec((1, tq, BRANCH_W), lambda b, r, i: (b, i, r)),
        ],
        out_shape=[
            jax.ShapeDtypeStruct((B, L, dil * BRANCH_W), BF16),
            jax.ShapeDtypeStruct((B, L, dil * BRANCH_W), F32),
        ],
        compiler_params=_params("arbitrary", "arbitrary", "arbitrary"),
        name=f"dilated_d{dil}",
    )(view, view, view)
    return o.reshape(B, S, BRANCH_W), lse.reshape(B, S, BRANCH_W)


def _nbr_kernel(q_ref, k_ref, v_ref, bias_ref, o_ref):
    S = k_ref.shape[1]
    rows = S // GRID_W
    t = pl.program_id(1)
    ks_row = jnp.clip(B_QROWS * t - NA_ROWS // 2, 0, rows - B_KROWS)
    ks = pl.multiple_of(ks_row * GRID_W, GRID_W)
    kw = k_ref[0, pl.ds(ks, B_KROWS * GRID_W), :]
    vw = v_ref[0, pl.ds(ks, B_KROWS * GRID_W), :]

    def add_bias(h, s):
        return s + bias_ref[0, h]

    o, _ = _attend_heads(q_ref[0], kw, vw, add_bias, False)
    o_ref[0] = o.astype(BF16)


def _nbr_variant(t, n_steps):
    return jnp.where(t < 2, t, jnp.where(t >= n_steps - 2, t - (n_steps - 5), 2))


def _nbr_bias(rpb, S):
    rows = S // GRID_W
    n_steps = rows // B_QROWS
    steps = np.array([0, 1, 2, n_steps - 2, n_steps - 1])
    r = (steps[:, None] * B_QROWS + np.arange(B_QROWS)[None, :])
    ks_row = np.clip(steps * B_QROWS - NA_ROWS // 2, 0, rows - B_KROWS)
    kabs = ks_row[:, None] + np.arange(B_KROWS)[None, :]
    rs = np.clip(r - NA_ROWS // 2, 0, rows - NA_ROWS)
    row_ok = (kabs[:, None, :] >= rs[:, :, None]) & (kabs[:, None, :] < rs[:, :, None] + NA_ROWS)
    dr = np.clip(kabs[:, None, :] - r[:, :, None] + NA_ROWS - 1, 0, 2 * NA_ROWS - 2)
    c = np.arange(GRID_W)
    col_start = np.clip(c - NA_COLS // 2, 0, GRID_W - NA_COLS)
    col_ok = (c[None, :] >= col_start[:, None]) & (c[None, :] < col_start[:, None] + NA_COLS)
    dc = np.clip(c[None, :] - c[:, None] + NA_COLS - 1, 0, 2 * NA_COLS - 2)
    DR = np.broadcast_to(dr[:, :, None, :, None], (5, B_QROWS, GRID_W, B_KROWS, GRID_W))
    DC = np.broadcast_to(dc[None, None, :, None, :], DR.shape)
    OK = row_ok[:, :, None, :, None] & col_ok[None, None, :, None, :]
    vals = rpb.astype(F32)[:, DR, DC]
    vals = jnp.where(jnp.asarray(OK)[None], vals, NEG_INF)
    H = rpb.shape[0]
    return vals.reshape(H, 5, B_QROWS * GRID_W, B_KROWS * GRID_W).transpose(1, 0, 2, 3)


def _neighbourhood(proj, bias):
    B, S, _ = proj.shape
    tq = B_QROWS * GRID_W
    n_steps = S // tq
    return pl.pallas_call(
        _nbr_kernel,
        grid=(B, n_steps),
        in_specs=[
            pl.BlockSpec((1, tq, BRANCH_W), lambda b, t: (b, t, CB_BQ)),
            pl.BlockSpec((1, S, BRANCH_W), lambda b, t: (b, 0, CB_BK)),
            pl.BlockSpec((1, S, BRANCH_W), lambda b, t: (b, 0, CB_BV)),
            pl.BlockSpec((1, 4, tq, B_KROWS * GRID_W),
                         lambda b, t: (_nbr_variant(t, n_steps), 0, 0, 0)),
        ],
        out_specs=pl.BlockSpec((1, tq, BRANCH_W), lambda b, t: (b, t, 0)),
        out_shape=jax.ShapeDtypeStruct((B, S, BRANCH_W), BF16),
        compiler_params=_params("arbitrary", "arbitrary"),
        name="neighbourhood",
    )(proj, proj, proj, bias)


def _merge_kernel(x_ref, oa_ref, ob_ref, od_ref, oc1_ref, oc2_ref, oc3_ref, l1_ref, l2_ref, l3_ref,
                  g0_ref, g1_ref, g2_ref, g3_ref, wb_ref, wo_ref, y_ref):
    l1, l2, l3 = l1_ref[0], l2_ref[0], l3_ref[0]
    m = jnp.maximum(jnp.maximum(l1, l2), l3)
    e1, e2, e3 = jnp.exp(l1 - m), jnp.exp(l2 - m), jnp.exp(l3 - m)
    oc = (e1 * oc1_ref[0].astype(F32) + e2 * oc2_ref[0].astype(F32)
          + e3 * oc3_ref[0].astype(F32)) / (e1 + e2 + e3)

    def branch(o, g_ref, i):
        return g_ref[0].astype(F32) * jnp.dot(o, wb_ref[i], preferred_element_type=F32)

    merged = branch(oa_ref[0], g0_ref, 0)
    merged = merged + branch(ob_ref[0], g1_ref, 1)
    merged = merged + branch(oc.astype(BF16), g2_ref, 2)
    merged = merged + branch(od_ref[0], g3_ref, 3)
    y_ref[0] = x_ref[0] + jnp.dot(merged.astype(BF16), wo_ref[...], preferred_element_type=F32)


def _merge(x, proj, o_a, o_b, o_d, oc, lse, w_branch, w_out):
    B, S, _ = x.shape
    tm = min(TM_MERGE, S)
    tok256 = pl.BlockSpec((1, tm, BRANCH_W), lambda b, i: (b, i, 0))

    def gate_spec(g):
        return pl.BlockSpec((1, tm, D_MODEL), lambda b, i: (b, i, g))

    return pl.pallas_call(
        _merge_kernel,
        grid=(B, S // tm),
        in_specs=[pl.BlockSpec((1, tm, D_MODEL), lambda b, i: (b, i, 0)),
                  tok256, tok256, tok256, tok256, tok256, tok256, tok256, tok256, tok256,
                  gate_spec(0), gate_spec(1), gate_spec(2), gate_spec(3),
                  pl.BlockSpec((N_BRANCH, BRANCH_W, D_MODEL), lambda b, i: (0, 0, 0)),
                  pl.BlockSpec((D_MODEL, D_MODEL), lambda b, i: (0, 0))],
        out_specs=pl.BlockSpec((1, tm, D_MODEL), lambda b, i: (b, i, 0)),
        out_shape=jax.ShapeDtypeStruct((B, S, D_MODEL), F32),
        compiler_params=_params("arbitrary", "arbitrary"),
        name="merge_out",
    )(x, o_a, o_b, o_d, oc[0], oc[1], oc[2], lse[0], lse[1], lse[2],
      proj, proj, proj, proj, w_branch, w_out)


def _mlp_kernel(x_ref, xp_ref, xn_ref, g_ref, wv_ref, wg_ref, cwv_ref, cwg_ref, cbv_ref, cbg_ref,
                wd_ref, gf_ref, y_ref, h_ref, hp_ref, hn_ref, acc_ref, *, final_norm):
    i = pl.program_id(1)
    k = pl.program_id(2)
    tm = x_ref.shape[1]

    def norm(x):
        ms = jnp.mean(x * x, axis=-1, keepdims=True)
        return (x * lax.rsqrt(ms + EPS)) * g_ref[...]

    @pl.when(k == 0)
    def _():
        h_ref[...] = norm(x_ref[0]).astype(BF16)
        hp_ref[...] = jnp.where(i > 0, norm(xp_ref[0]), 0.0).astype(BF16)
        hn_ref[...] = jnp.where(i < pl.num_programs(1) - 1, norm(xn_ref[0]), 0.0).astype(BF16)
        acc_ref[...] = jnp.zeros_like(acc_ref)

    def conv_half(w_ref, cw_ref, cb_ref):
        w = w_ref[...]
        u = jnp.dot(h_ref[...], w, preferred_element_type=F32)
        u_prev = jnp.dot(hp_ref[...], w, preferred_element_type=F32)
        u_next = jnp.dot(hn_ref[...], w, preferred_element_type=F32)
        row = lax.broadcasted_iota(jnp.int32, u.shape, 0)
        before = jnp.where(row == 0, u_prev[7:8, :], pltpu.roll(u, 1, 0))
        after = jnp.where(row == tm - 1, u_next[0:1, :], pltpu.roll(u, tm - 1, 0))
        return before * cw_ref[0:1, :] + u * cw_ref[1:2, :] + after * cw_ref[2:3, :] + cb_ref[...]

    val = conv_half(wv_ref, cwv_ref, cbv_ref)
    gt = conv_half(wg_ref, cwg_ref, cbg_ref)
    act = (0.5 * gt * (1.0 + lax.erf(gt * math.sqrt(0.5)))) * val
    acc_ref[...] += jnp.dot(act.astype(BF16), wd_ref[...], preferred_element_type=F32)

    @pl.when(k == pl.num_programs(2) - 1)
    def _():
        y = x_ref[0] + acc_ref[...]
        if final_norm:
            ms = jnp.mean(y * y, axis=-1, keepdims=True)
            y = (y * lax.rsqrt(ms + EPS)) * gf_ref[...]
        y_ref[0] = y


def _mlp(x, gain, w_up, conv_w, conv_b, w_down, gain_final, final_norm):
    B, S, _ = x.shape
    tm = min(TM_MLP, S)
    c = FF_CHUNK
    nck = D_FF // c
    rows8 = tm // 8
    n_i = S // tm
    kern = functools.partial(_mlp_kernel, final_norm=final_norm)
    return pl.pallas_call(
        kern,
        grid=(B, n_i, nck),
        in_specs=[
            pl.BlockSpec((1, tm, D_MODEL), lambda b, i, k: (b, i, 0)),
            pl.BlockSpec((1, 8, D_MODEL), lambda b, i, k: (b, jnp.maximum(i * rows8 - 1, 0), 0)),
            pl.BlockSpec((1, 8, D_MODEL), lambda b, i, k: (b, jnp.minimum((i + 1) * rows8, S // 8 - 1), 0)),
            pl.BlockSpec((1, D_MODEL), lambda b, i, k: (0, 0)),
            pl.BlockSpec((D_MODEL, c), lambda b, i, k: (0, k)),
            pl.BlockSpec((D_MODEL, c), lambda b, i, k: (0, nck + k)),
            pl.BlockSpec((3, c), lambda b, i, k: (0, k)),
            pl.BlockSpec((3, c), lambda b, i, k: (0, nck + k)),
            pl.BlockSpec((1, c), lambda b, i, k: (0, k)),
            pl.BlockSpec((1, c), lambda b, i, k: (0, nck + k)),
            pl.BlockSpec((c, D_MODEL), lambda b, i, k: (k, 0)),
            pl.BlockSpec((1, D_MODEL), lambda b, i, k: (0, 0)),
        ],
        out_specs=pl.BlockSpec((1, tm, D_MODEL), lambda b, i, k: (b, i, 0)),
        out_shape=jax.ShapeDtypeStruct((B, S, D_MODEL), F32),
        scratch_shapes=[
            pltpu.VMEM((tm, D_MODEL), BF16),
            pltpu.VMEM((8, D_MODEL), BF16),
            pltpu.VMEM((8, D_MODEL), BF16),
            pltpu.VMEM((tm, D_MODEL), F32),
        ],
        compiler_params=_params("arbitrary", "arbitrary", "arbitrary"),
        name="mlp",
    )(x, x, x, gain, w_up, w_up, conv_w, conv_w, conv_b, conv_b, w_down, gain_final)


def _layer(x, l, tables, gmat, p):
    lam_init = 0.8 - 0.6 * math.exp(-0.3 * l)
    S = x.shape[1]
    w_in = p["w_in"][l]
    w_in = jnp.concatenate([w_in[:, N_MIX_COLS:], w_in[:, :N_MIX_COLS]], axis=1).astype(BF16)
    qk_gain = jnp.stack([jnp.tile(p["qk_norm"][l, 0], 4), jnp.tile(p["qk_norm"][l, 1], 4)]).astype(F32)
    proj = _in_proj(x, p["norm_attn"][l][None, :], w_in, tables, gmat, qk_gain)

    lv = p["diff_lambda"][l].astype(F32)
    lam = (jnp.exp(jnp.sum(lv[0] * lv[1])) - jnp.exp(jnp.sum(lv[2] * lv[3])) + lam_init).reshape(1)
    tq = min(TQ_FULL, S)
    sub_gain = jnp.broadcast_to(p["diff_subln"][l].astype(F32)[:, None], (HEAD_DIM, tq))
    o_a = _full_attention(proj, lam, sub_gain, q_cb=CB_AQ, k_cb=CB_AK, v_cb=CB_AV, heads=_HEADS_A,
                          v_row0=0, diff=True, post_scale=1.0 - lam_init)
    o_d = _full_attention(proj, lam, sub_gain, q_cb=CB_DQ, k_cb=CB_DKV, v_cb=CB_DKV, heads=_HEADS_D,
                          v_row0=2 * HEAD_DIM, diff=False, post_scale=1.0)
    o_b = _neighbourhood(proj, _nbr_bias(p["na_rpb"][l], S))
    oc, lse = zip(*[_dilated_pattern(proj, dil) for _, dil in C_PATTERNS])
    x = _merge(x, proj, o_a, o_b, o_d, oc, lse, p["w_branch"][l].astype(BF16), p["w_out"][l].astype(BF16))
    return _mlp(x, p["norm_mlp"][l][None, :], p["w_up"][l].astype(BF16), p["conv_w"][l],
                p["conv_b"][l][None, :], p["w_down"][l].astype(BF16), p["norm_final"][None, :],
                final_norm=(l == DEPTH - 1))


def _trunk(x, tables, gmat, p):
    for l in range(DEPTH):
        x = _layer(x, l, tables, gmat, p)
    return x


def _group_mean_matrix():
    head = np.arange(BRANCH_W) // HEAD_DIM
    return jnp.asarray((head[:, None] == head[None, :]) / HEAD_DIM, dtype=BF16)


def kernel(x_prompt, x_sample, norm_attn, w_in, diff_lambda, diff_subln, na_rpb, qk_norm, w_branch,
           w_out, norm_mlp, w_up, conv_w, conv_b, w_down, norm_final):
    p = dict(norm_attn=norm_attn, w_in=w_in, diff_lambda=diff_lambda, diff_subln=diff_subln,
             na_rpb=na_rpb, qk_norm=qk_norm, w_branch=w_branch, w_out=w_out, norm_mlp=norm_mlp,
             w_up=w_up, conv_w=conv_w, conv_b=conv_b, w_down=w_down, norm_final=norm_final)
    gmat = _group_mean_matrix()
    outs = []
    for x in (x_prompt, x_sample):
        tables = _rope_tables(x.shape[1])
        outs.append(_trunk(x, tables, gmat, p))
    return tuple(outs)
```

```python
import functools
import math

import jax
import jax.numpy as jnp
import numpy as np
from jax import lax
from jax.experimental import pallas as pl
from jax.experimental.pallas import tpu as pltpu

F32 = jnp.float32
BF16 = jnp.bfloat16

D_MODEL = 1024
HEAD_DIM = 64
DIFF_DIM = 32
BRANCH_W = 256
N_BRANCH = 4
GRID_W = 64
NA_ROWS = 8
NA_COLS = 16
C_PATTERNS = ((128, 1), (512, 4), (2048, 16))
ROPE_THETA = 500000.0
AXIAL_THETA = 10000.0
D_FF = 2816
EPS = 1e-6
NEG_INF = -1e30
DEPTH = 2
IN_COLS = 6912
N_COL_BLOCKS = IN_COLS // BRANCH_W
N_GATE_BLOCKS = N_BRANCH * D_MODEL // BRANCH_W
N_MIX_COLS = IN_COLS - N_BRANCH * D_MODEL
LOG2E = 1.4426950408889634

(CB_AQ, CB_AK, CB_AV, CB_BQ, CB_BK, CB_BV, CB_CQ, CB_CK, CB_CV, CB_DQ,
 CB_DKV) = range(N_GATE_BLOCKS, N_GATE_BLOCKS + 11)

VMEM_LIMIT = 56 * 1024 * 1024

TM_PROJ = 1024
PROJ_ROW_CHUNK = 256
TQ_FULL = 512
TK_FULL = 256
ONES_ROWS = 16
QK_LOOKAHEAD = 3
SCORE_SLOTS = QK_LOOKAHEAD + 1
TQ_WIN = 256
C_HALF = 64
B_QROWS = 2
B_KROWS = 10
TM_MERGE = 512
TM_MLP = 1024
FF_CHUNK = 256


def _params(*sem):
    return pltpu.CompilerParams(dimension_semantics=sem, vmem_limit_bytes=VMEM_LIMIT)


def _rotate_pairs(y, cos, sin, half, group):
    lane = lax.broadcasted_iota(jnp.int32, y.shape, 1) % group
    second = (lane >= half) & (lane < 2 * half)
    from_below = pltpu.roll(y, half, 1)
    from_above = pltpu.roll(y, y.shape[1] - half, 1)
    return y * cos + jnp.where(second, from_below, from_above) * sin


def _head_rms(y, gmat, gain):
    sq = y * y
    hi = sq.astype(BF16)
    lo = (sq - hi.astype(F32)).astype(BF16)
    ms = (jnp.dot(hi, gmat, preferred_element_type=F32)
          + jnp.dot(lo, gmat, preferred_element_type=F32))
    return y * lax.rsqrt(ms + EPS) * gain


def _in_proj_kernel(x_ref, g_ref, w_ref, tab_ref, gmat_ref, qkg_ref, o_ref, h_ref):
    j = pl.program_id(2)

    @pl.when(j == 0)
    def _():
        x = x_ref[0]
        ms = jnp.mean(x * x, axis=-1, keepdims=True)
        h_ref[...] = ((x * lax.rsqrt(ms + EPS)) * g_ref[...]).astype(BF16)

    scale_a = DIFF_DIM ** -0.5 * LOG2E
    scale_d = HEAD_DIM ** -0.5 * LOG2E
    scale_w = HEAD_DIM ** -0.5
    tm = h_ref.shape[0]
    rc = min(PROJ_ROW_CHUNK, tm)

    def project(epilogue):
        for c in range(tm // rc):
            rows = slice(c * rc, (c + 1) * rc)
            acc = jnp.dot(h_ref[rows, :], w_ref[...], preferred_element_type=F32)
            o_ref[0, rows, :] = epilogue(acc, rows).astype(BF16)

    def rope_a(y, rows):
        return _rotate_pairs(y, tab_ref[0, rows, :], tab_ref[1, rows, :], 4, 32)

    def rope_c(y, rows):
        return _rotate_pairs(y, tab_ref[2, rows, :], tab_ref[3, rows, :], 8, 64)

    def rope_d(y, rows):
        return _rotate_pairs(y, tab_ref[4, rows, :], tab_ref[5, rows, :], 16, 32)

    @pl.when(j == CB_AQ)
    def _():
        project(lambda acc, rows: rope_a(acc, rows) * scale_a)

    @pl.when(j == CB_AK)
    def _():
        project(rope_a)

    @pl.when((j == CB_AV) | (j == CB_BK) | (j == CB_BV) | (j == CB_CV))
    def _():
        project(lambda acc, rows: acc)

    @pl.when(j == CB_BQ)
    def _():
        project(lambda acc, rows: acc * scale_w)

    @pl.when(j == CB_CQ)
    def _():
        project(lambda acc, rows: rope_c(acc, rows) * scale_w)

    @pl.when(j == CB_CK)
    def _():
        project(rope_c)

    @pl.when(j == CB_DQ)
    def _():
        project(lambda acc, rows:
                rope_d(_head_rms(acc, gmat_ref[...], qkg_ref[0:1, :]), rows) * scale_d)

    @pl.when(j == CB_DKV)
    def _():
        def keys_and_values(acc, rows):
            lane = lax.broadcasted_iota(jnp.int32, acc.shape, 1)
            keys = rope_d(_head_rms(acc, gmat_ref[...], qkg_ref[1:2, :]), rows)
            return jnp.where(lane < 2 * HEAD_DIM, keys, acc)

        project(keys_and_values)

    @pl.when(j < N_GATE_BLOCKS)
    def _():
        project(lambda acc, rows: 0.5 + 0.5 * jnp.tanh(0.5 * acc))


def _in_proj(x, gain, w_bf16, tables, gmat, qk_gain):
    B, S, _ = x.shape
    tm = min(TM_PROJ, S)
    grid = (S // tm, B, N_COL_BLOCKS)
    return pl.pallas_call(
        _in_proj_kernel,
        grid=grid,
        in_specs=[
            pl.BlockSpec((1, tm, D_MODEL), lambda i, b, j: (b, i, 0)),
            pl.BlockSpec((1, D_MODEL), lambda i, b, j: (0, 0)),
            pl.BlockSpec((D_MODEL, BRANCH_W), lambda i, b, j: (0, j)),
            pl.BlockSpec((6, tm, BRANCH_W), lambda i, b, j: (0, i, 0)),
            pl.BlockSpec((BRANCH_W, BRANCH_W), lambda i, b, j: (0, 0)),
            pl.BlockSpec((2, BRANCH_W), lambda i, b, j: (0, 0)),
        ],
        out_specs=pl.BlockSpec((1, tm, BRANCH_W), lambda i, b, j: (b, i, j)),
        out_shape=jax.ShapeDtypeStruct((B, S, IN_COLS), BF16),
        scratch_shapes=[pltpu.VMEM((tm, D_MODEL), BF16)],
        compiler_params=_params("arbitrary", "arbitrary", "arbitrary"),
        name="in_proj",
    )(x, gain, w_bf16, tables, gmat, qk_gain)


def _rope_tables(S):
    pos = jnp.arange(S, dtype=jnp.int32)
    lane = np.arange(BRANCH_W)

    def build(group, half, ang_of_lane):
        m = lane % group
        first = m < half
        second = (m >= half) & (m < 2 * half)
        ang = ang_of_lane
        cos = jnp.where(jnp.asarray(first | second)[None, :], jnp.cos(ang), 1.0)
        sin = jnp.where(jnp.asarray(second)[None, :], jnp.sin(ang),
                        jnp.where(jnp.asarray(first)[None, :], -jnp.sin(ang), 0.0))
        return cos.astype(F32), sin.astype(F32)

    def angles(p, theta, half, idx):
        inv = jnp.exp(-math.log(theta) * jnp.arange(half, dtype=F32) / half)
        ang = p.astype(F32)[:, None] * inv[None, :]
        return ang[:, idx]

    ca, sa = build(32, 4, angles(pos, ROPE_THETA, 4, (lane % 32) % 4))
    cc, sc = build(64, 8, angles(pos, ROPE_THETA, 8, (lane % 64) % 8))
    idx = (lane % 32) % 16
    ang_row = angles(pos // GRID_W, AXIAL_THETA, 16, idx)
    ang_col = angles(pos % GRID_W, AXIAL_THETA, 16, idx)
    ang_d = jnp.where(jnp.asarray((lane % 64) < 32)[None, :], ang_row, ang_col)
    cd, sd = build(32, 16, ang_d)
    return jnp.stack([ca, sa, cc, sc, cd, sd], axis=0)


def _full_attn_kernel(lam_ref, q_ref, k_ref, v_ref, sub_ref, o_ref, vt_ref, qpad_ref, m_ref, acc_ref,
                      ot_ref, s_ref, *, heads, n_kv, v_row0, unroll, diff, post_scale):
    S = k_ref.shape[1]
    tq = q_ref.shape[1]
    tk = min(TK_FULL, S)
    n_kt = S // tk
    n_vh = len(heads)
    vrows = vt_ref.shape[1]

    @pl.when(pl.program_id(1) == 0)
    def _():
        chunk = min(512, S)

        def tr(c, carry):
            r0 = pl.multiple_of(c * chunk, chunk)
            vct = v_ref[0, pl.ds(r0, chunk), :].astype(F32).T
            for g in range(n_kv):
                lo = v_row0 + HEAD_DIM * g
                vt_ref[g, 0:HEAD_DIM, pl.ds(r0, chunk)] = vct[lo:lo + HEAD_DIM, :].astype(BF16)
                vt_ref[g, HEAD_DIM:vrows, pl.ds(r0, chunk)] = jnp.ones((vrows - HEAD_DIM, chunk), BF16)
            return carry

        lax.fori_loop(0, S // chunk, tr, 0)

    qt = q_ref[0].astype(F32).T.astype(BF16)
    for vh, (slo, shi, dlo, _) in enumerate(heads):
        qpad_ref[vh] = jnp.zeros((BRANCH_W, tq), BF16)
        qpad_ref[vh, dlo:dlo + (shi - slo), :] = qt[slo:shi, :]
    m_ref[...] = jnp.full(m_ref.shape, NEG_INF, F32)
    acc_ref[...] = jnp.zeros(acc_ref.shape, F32)

    n_steps = unroll * n_vh
    n_slot = s_ref.shape[0]
    assert n_steps % n_slot == 0 and QK_LOOKAHEAD < n_slot and n_kt % unroll == 0

    def key_rows(it, step):
        kt = jnp.minimum(it * unroll + step // n_vh, n_kt - 1)
        return pl.multiple_of(kt * tk, tk)

    def scores(it, step):
        s_ref[step % n_slot] = jnp.dot(k_ref[0, pl.ds(key_rows(it, step), tk), :],
                                       qpad_ref[step % n_vh], preferred_element_type=F32)

    for step in range(QK_LOOKAHEAD):
        scores(0, step)

    def body(it, carry):
        for step in range(n_steps):
            vh = step % n_vh
            scores(it, step + QK_LOOKAHEAD)
            s = s_ref[step % n_slot]
            m_old = m_ref[vh]
            m_new = jnp.maximum(m_old, jnp.max(s, axis=0, keepdims=True))
            alpha = jnp.exp2(m_old - m_new)
            p = jnp.exp2(s - m_new).astype(BF16)
            vt = vt_ref[heads[vh][3], :, pl.ds(key_rows(it, step), tk)]
            acc_ref[vh] = alpha * acc_ref[vh] + jnp.dot(vt, p, preferred_element_type=F32)
            m_ref[vh] = m_new
        return carry

    lax.fori_loop(0, n_kt // unroll, body, 0)

    def head_out(vh):
        a = acc_ref[vh]
        return a[0:HEAD_DIM, :] / a[HEAD_DIM:HEAD_DIM + 1, :]

    if diff:
        lam = lam_ref[0]
        for h in range(n_vh // 2):
            o = head_out(2 * h) - lam * head_out(2 * h + 1)
            ms = jnp.mean(o * o, axis=0, keepdims=True)
            y = (o * lax.rsqrt(ms + EPS)) * sub_ref[...]
            ot_ref[HEAD_DIM * h:HEAD_DIM * (h + 1), :] = y * post_scale
    else:
        for h in range(n_vh):
            ot_ref[HEAD_DIM * h:HEAD_DIM * (h + 1), :] = head_out(h)
    o_ref[0] = ot_ref[...].T.astype(BF16)


def _full_attention(proj, lam, sub_gain, *, q_cb, k_cb, v_cb, heads, n_kv, v_row0, unroll, diff,
                    post_scale):
    B, S, _ = proj.shape
    tq = min(TQ_FULL, S)
    kern = functools.partial(_full_attn_kernel, heads=heads, n_kv=n_kv, v_row0=v_row0,
                             unroll=unroll, diff=diff, post_scale=post_scale)
    return pl.pallas_call(
        kern,
        grid=(B, S // tq),
        in_specs=[
            pl.BlockSpec(memory_space=pltpu.SMEM),
            pl.BlockSpec((1, tq, BRANCH_W), lambda b, i: (b, i, q_cb)),
            pl.BlockSpec((1, S, BRANCH_W), lambda b, i: (b, 0, k_cb)),
            pl.BlockSpec((1, S, BRANCH_W), lambda b, i: (b, 0, v_cb)),
            pl.BlockSpec((HEAD_DIM, tq), lambda b, i: (0, 0)),
        ],
        out_specs=pl.BlockSpec((1, tq, BRANCH_W), lambda b, i: (b, i, 0)),
        out_shape=jax.ShapeDtypeStruct((B, S, BRANCH_W), BF16),
        scratch_shapes=[
            pltpu.VMEM((n_kv, HEAD_DIM + ONES_ROWS, S), BF16),
            pltpu.VMEM((len(heads), BRANCH_W, tq), BF16),
            pltpu.VMEM((len(heads), 1, tq), F32),
            pltpu.VMEM((len(heads), HEAD_DIM + ONES_ROWS, tq), F32),
            pltpu.VMEM((BRANCH_W, tq), F32),
            pltpu.VMEM((SCORE_SLOTS, min(TK_FULL, S), tq), F32),
        ],
        compiler_params=_params("arbitrary", "arbitrary"),
        name="full_attn_diff" if diff else "full_attn_gqa",
    )(lam, proj, proj, proj, sub_gain)


_HEADS_A = tuple((HEAD_DIM * h + DIFF_DIM * c, HEAD_DIM * h + DIFF_DIM * (c + 1),
                  HEAD_DIM * h + DIFF_DIM * c, h) for h in range(4) for c in range(2))
_HEADS_D = tuple((HEAD_DIM * h, HEAD_DIM * (h + 1), HEAD_DIM * (h // 2), h // 2) for h in range(4))


def _attend_heads(q, kw, vw, bias_of_head, want_lse):
    tq = q.shape[0]
    lane = lax.broadcasted_iota(jnp.int32, (tq, BRANCH_W), 1)
    o = jnp.zeros((tq, BRANCH_W), F32)
    lse = jnp.zeros((tq, BRANCH_W), F32)
    in_head = [(lane >= HEAD_DIM * h) & (lane < HEAD_DIM * (h + 1)) for h in range(4)]
    scores = [lax.dot_general(jnp.where(in_head[h], q, jnp.zeros_like(q)), kw,
                              (((1,), (1,)), ((), ())), preferred_element_type=F32)
              for h in range(4)]
    stats = []
    for h in range(4):
        s = bias_of_head(h, scores[h])
        m = jnp.max(s, axis=-1, keepdims=True)
        p = jnp.exp(s - m)
        stats.append((m, jnp.sum(p, axis=-1, keepdims=True), p.astype(BF16)))
    for h in range(4):
        m, l, p = stats[h]
        of = jnp.dot(p, vw, preferred_element_type=F32)
        o = jnp.where(in_head[h], of / l, o)
        if want_lse:
            lse = jnp.where(in_head[h], m + jnp.log(l), lse)
    return o, lse


def _band_kernel(q_ref, k_ref, v_ref, o_ref, lse_ref):
    L = k_ref.shape[1]
    tq = q_ref.shape[1]
    kwin = min(tq + 2 * C_HALF, L)
    q0 = pl.program_id(2) * tq
    ks = pl.multiple_of(jnp.clip(q0 - C_HALF, 0, L - kwin), C_HALF)
    kw = k_ref[0, pl.ds(ks, kwin), :]
    vw = v_ref[0, pl.ds(ks, kwin), :]
    qpos = q0 + lax.broadcasted_iota(jnp.int32, (tq, kwin), 0)
    kpos = ks + lax.broadcasted_iota(jnp.int32, (tq, kwin), 1)
    valid = jnp.abs(qpos - kpos) <= C_HALF

    def mask(h, s):
        return jnp.where(valid, s, NEG_INF)

    o, lse = _attend_heads(q_ref[0], kw, vw, mask, True)
    o_ref[0] = o.astype(BF16)
    lse_ref[0] = lse


def _dilated_pattern(proj, dil):
    B, S, _ = proj.shape
    L = S // dil
    tq = min(TQ_WIN, L)
    if dil == 1:
        view, nb, cb0 = proj, N_COL_BLOCKS, CB_CQ
    else:
        qkv = proj[:, :, CB_CQ * BRANCH_W:(CB_CV + 1) * BRANCH_W]
        view, nb, cb0 = qkv.reshape(B, L, dil * 3 * BRANCH_W), 3, 0
    o, lse = pl.pallas_call(
        _band_kernel,
        grid=(B, dil, L // tq),
        in_specs=[
            pl.BlockSpec((1, tq, BRANCH_W), lambda b, r, i: (b, i, r * nb + cb0)),
            pl.BlockSpec((1, L, BRANCH_W), lambda b, r, i: (b, 0, r * nb + cb0 + 1)),
            pl.BlockSpec((1, L, BRANCH_W), lambda b, r, i: (b, 0, r * nb + cb0 + 2)),
        ],
        out_specs=[
            pl.BlockSpec((1, tq, BRANCH_W), lambda b, r, i: (b, i, r)),
            pl.BlockSpec((1, tq, BRANCH_W), lambda b, r, i: (b, i, r)),
        ],
        out_shape=[
            jax.ShapeDtypeStruct((B, L, dil * BRANCH_W), BF16),
            jax.ShapeDtypeStruct((B, L, dil * BRANCH_W), F32),
        ],
        compiler_params=_params("arbitrary", "arbitrary", "arbitrary"),
        name=f"dilated_d{dil}",
    )(view, view, view)
    return o.reshape(B, S, BRANCH_W), lse.reshape(B, S, BRANCH_W)


def _nbr_kernel(q_ref, k_ref, v_ref, bias_ref, o_ref):
    S = k_ref.shape[1]
    rows = S // GRID_W
    t = pl.program_id(1)
    ks_row = jnp.clip(B_QROWS * t - NA_ROWS // 2, 0, rows - B_KROWS)
    ks = pl.multiple_of(ks_row * GRID_W, GRID_W)
    kw = k_ref[0, pl.ds(ks, B_KROWS * GRID_W), :]
    vw = v_ref[0, pl.ds(ks, B_KROWS * GRID_W), :]

    def add_bias(h, s):
        return s + bias_ref[0, h]

    o, _ = _attend_heads(q_ref[0], kw, vw, add_bias, False)
    o_ref[0] = o.astype(BF16)


def _nbr_variant(t, n_steps):
    return jnp.where(t < 2, t, jnp.where(t >= n_steps - 2, t - (n_steps - 5), 2))


def _nbr_bias(rpb, S):
    rows = S // GRID_W
    n_steps = rows // B_QROWS
    steps = np.array([0, 1, 2, n_steps - 2, n_steps - 1])
    r = (steps[:, None] * B_QROWS + np.arange(B_QROWS)[None, :])
    ks_row = np.clip(steps * B_QROWS - NA_ROWS // 2, 0, rows - B_KROWS)
    kabs = ks_row[:, None] + np.arange(B_KROWS)[None, :]
    rs = np.clip(r - NA_ROWS // 2, 0, rows - NA_ROWS)
    row_ok = (kabs[:, None, :] >= rs[:, :, None]) & (kabs[:, None, :] < rs[:, :, None] + NA_ROWS)
    dr = np.clip(kabs[:, None, :] - r[:, :, None] + NA_ROWS - 1, 0, 2 * NA_ROWS - 2)
    c = np.arange(GRID_W)
    col_start = np.clip(c - NA_COLS // 2, 0, GRID_W - NA_COLS)
    col_ok = (c[None, :] >= col_start[:, None]) & (c[None, :] < col_start[:, None] + NA_COLS)
    dc = np.clip(c[None, :] - c[:, None] + NA_COLS - 1, 0, 2 * NA_COLS - 2)
    H = rpb.shape[0]
    onehot = (dc.reshape(-1)[None, :] == np.arange(2 * NA_COLS - 1)[:, None]).astype(np.float32)
    by_col = jnp.einsum("hdc,cn->hdn", rpb.astype(F32), jnp.asarray(onehot),
                        precision=lax.Precision.HIGHEST).reshape(H, 2 * NA_ROWS - 1, GRID_W, GRID_W)
    vals = jnp.stack([by_col[:, int(d)] for d in dr.reshape(-1)], axis=1)
    vals = vals.reshape(H, 5, B_QROWS, B_KROWS, GRID_W, GRID_W)
    vals = vals.transpose(0, 1, 2, 4, 3, 5)
    OK = row_ok[:, :, None, :, None] & col_ok[None, None, :, None, :]
    vals = jnp.where(jnp.asarray(OK)[None], vals, NEG_INF)
    return vals.reshape(H, 5, B_QROWS * GRID_W, B_KROWS * GRID_W).transpose(1, 0, 2, 3)


def _neighbourhood(proj, bias):
    B, S, _ = proj.shape
    tq = B_QROWS * GRID_W
    n_steps = S // tq
    return pl.pallas_call(
        _nbr_kernel,
        grid=(B, n_steps),
        in_specs=[
            pl.BlockSpec((1, tq, BRANCH_W), lambda b, t: (b, t, CB_BQ)),
            pl.BlockSpec((1, S, BRANCH_W), lambda b, t: (b, 0, CB_BK)),
            pl.BlockSpec((1, S, BRANCH_W), lambda b, t: (b, 0, CB_BV)),
            pl.BlockSpec((1, 4, tq, B_KROWS * GRID_W),
                         lambda b, t: (_nbr_variant(t, n_steps), 0, 0, 0)),
        ],
        out_specs=pl.BlockSpec((1, tq, BRANCH_W), lambda b, t: (b, t, 0)),
        out_shape=jax.ShapeDtypeStruct((B, S, BRANCH_W), BF16),
        compiler_params=_params("arbitrary", "arbitrary"),
        name="neighbourhood",
    )(proj, proj, proj, bias)


def _merge_kernel(x_ref, oa_ref, ob_ref, od_ref, oc1_ref, oc2_ref, oc3_ref, l1_ref, l2_ref, l3_ref,
                  g0_ref, g1_ref, g2_ref, g3_ref, wb_ref, wo_ref, y_ref):
    l1, l2, l3 = l1_ref[0], l2_ref[0], l3_ref[0]
    m = jnp.maximum(jnp.maximum(l1, l2), l3)
    e1, e2, e3 = jnp.exp(l1 - m), jnp.exp(l2 - m), jnp.exp(l3 - m)
    oc = (e1 * oc1_ref[0].astype(F32) + e2 * oc2_ref[0].astype(F32)
          + e3 * oc3_ref[0].astype(F32)) / (e1 + e2 + e3)

    def branch(o, g_ref, i):
        return g_ref[0].astype(F32) * jnp.dot(o, wb_ref[i], preferred_element_type=F32)

    merged = branch(oa_ref[0], g0_ref, 0)
    merged = merged + branch(ob_ref[0], g1_ref, 1)
    merged = merged + branch(oc.astype(BF16), g2_ref, 2)
    merged = merged + branch(od_ref[0], g3_ref, 3)
    y_ref[0] = x_ref[0] + jnp.dot(merged.astype(BF16), wo_ref[...], preferred_element_type=F32)


def _merge(x, proj, o_a, o_b, o_d, oc, lse, w_branch, w_out):
    B, S, _ = x.shape
    tm = min(TM_MERGE, S)
    tok256 = pl.BlockSpec((1, tm, BRANCH_W), lambda b, i: (b, i, 0))

    def gate_spec(g):
        return pl.BlockSpec((1, tm, D_MODEL), lambda b, i: (b, i, g))

    return pl.pallas_call(
        _merge_kernel,
        grid=(B, S // tm),
        in_specs=[pl.BlockSpec((1, tm, D_MODEL), lambda b, i: (b, i, 0)),
                  tok256, tok256, tok256, tok256, tok256, tok256, tok256, tok256, tok256,
                  gate_spec(0), gate_spec(1), gate_spec(2), gate_spec(3),
                  pl.BlockSpec((N_BRANCH, BRANCH_W, D_MODEL), lambda b, i: (0, 0, 0)),
                  pl.BlockSpec((D_MODEL, D_MODEL), lambda b, i: (0, 0))],
        out_specs=pl.BlockSpec((1, tm, D_MODEL), lambda b, i: (b, i, 0)),
        out_shape=jax.ShapeDtypeStruct((B, S, D_MODEL), F32),
        compiler_params=_params("arbitrary", "arbitrary"),
        name="merge_out",
    )(x, o_a, o_b, o_d, oc[0], oc[1], oc[2], lse[0], lse[1], lse[2],
      proj, proj, proj, proj, w_branch, w_out)


def _mlp_kernel(x_ref, xp_ref, xn_ref, g_ref, wv_ref, wg_ref, cwv_ref, cwg_ref, cbv_ref, cbg_ref,
                wd_ref, gf_ref, y_ref, h_ref, hp_ref, hn_ref, acc_ref, *, final_norm):
    i = pl.program_id(1)
    k = pl.program_id(2)
    tm = x_ref.shape[1]

    def norm(x):
        ms = jnp.mean(x * x, axis=-1, keepdims=True)
        return (x * lax.rsqrt(ms + EPS)) * g_ref[...]

    @pl.when(k == 0)
    def _():
        h_ref[...] = norm(x_ref[0]).astype(BF16)
        hp_ref[...] = jnp.where(i > 0, norm(xp_ref[0]), 0.0).astype(BF16)
        hn_ref[...] = jnp.where(i < pl.num_programs(1) - 1, norm(xn_ref[0]), 0.0).astype(BF16)
        acc_ref[...] = jnp.zeros_like(acc_ref)

    def conv_half(w_ref, cw_ref, cb_ref):
        w = w_ref[...]
        u = jnp.dot(h_ref[...], w, preferred_element_type=F32)
        u_prev = jnp.dot(hp_ref[...], w, preferred_element_type=F32)
        u_next = jnp.dot(hn_ref[...], w, preferred_element_type=F32)
        row = lax.broadcasted_iota(jnp.int32, u.shape, 0)
        before = jnp.where(row == 0, u_prev[7:8, :], pltpu.roll(u, 1, 0))
        after = jnp.where(row == tm - 1, u_next[0:1, :], pltpu.roll(u, tm - 1, 0))
        return before * cw_ref[0:1, :] + u * cw_ref[1:2, :] + after * cw_ref[2:3, :] + cb_ref[...]

    val = conv_half(wv_ref, cwv_ref, cbv_ref)
    gt = conv_half(wg_ref, cwg_ref, cbg_ref)
    act = (0.5 * gt * (1.0 + lax.erf(gt * math.sqrt(0.5)))) * val
    acc_ref[...] += jnp.dot(act.astype(BF16), wd_ref[...], preferred_element_type=F32)

    @pl.when(k == pl.num_programs(2) - 1)
    def _():
        y = x_ref[0] + acc_ref[...]
        if final_norm:
            ms = jnp.mean(y * y, axis=-1, keepdims=True)
            y = (y * lax.rsqrt(ms + EPS)) * gf_ref[...]
        y_ref[0] = y


def _mlp(x, gain, w_up, conv_w, conv_b, w_down, gain_final, final_norm):
    B, S, _ = x.shape
    tm = min(TM_MLP, S)
    c = FF_CHUNK
    nck = D_FF // c
    rows8 = tm // 8
    n_i = S // tm
    kern = functools.partial(_mlp_kernel, final_norm=final_norm)
    return pl.pallas_call(
        kern,
        grid=(B, n_i, nck),
        in_specs=[
            pl.BlockSpec((1, tm, D_MODEL), lambda b, i, k: (b, i, 0)),
            pl.BlockSpec((1, 8, D_MODEL), lambda b, i, k: (b, jnp.maximum(i * rows8 - 1, 0), 0)),
            pl.BlockSpec((1, 8, D_MODEL), lambda b, i, k: (b, jnp.minimum((i + 1) * rows8, S // 8 - 1), 0)),
            pl.BlockSpec((1, D_MODEL), lambda b, i, k: (0, 0)),
            pl.BlockSpec((D_MODEL, c), lambda b, i, k: (0, k)),
            pl.BlockSpec((D_MODEL, c), lambda b, i, k: (0, nck + k)),
            pl.BlockSpec((3, c), lambda b, i, k: (0, k)),
            pl.BlockSpec((3, c), lambda b, i, k: (0, nck + k)),
            pl.BlockSpec((1, c), lambda b, i, k: (0, k)),
            pl.BlockSpec((1, c), lambda b, i, k: (0, nck + k)),
            pl.BlockSpec((c, D_MODEL), lambda b, i, k: (k, 0)),
            pl.BlockSpec((1, D_MODEL), lambda b, i, k: (0, 0)),
        ],
        out_specs=pl.BlockSpec((1, tm, D_MODEL), lambda b, i, k: (b, i, 0)),
        out_shape=jax.ShapeDtypeStruct((B, S, D_MODEL), F32),
        scratch_shapes=[
            pltpu.VMEM((tm, D_MODEL), BF16),
            pltpu.VMEM((8, D_MODEL), BF16),
            pltpu.VMEM((8, D_MODEL), BF16),
            pltpu.VMEM((tm, D_MODEL), F32),
        ],
        compiler_params=_params("arbitrary", "arbitrary", "arbitrary"),
        name="mlp",
    )(x, x, x, gain, w_up, w_up, conv_w, conv_w, conv_b, conv_b, w_down, gain_final)


def _layer(x, l, tables, gmat, p):
    lam_init = 0.8 - 0.6 * math.exp(-0.3 * l)
    S = x.shape[1]
    w_in = p["w_in"][l]
    w_in = jnp.concatenate([w_in[:, N_MIX_COLS:], w_in[:, :N_MIX_COLS]], axis=1).astype(BF16)
    qk_gain = jnp.stack([jnp.tile(p["qk_norm"][l, 0], 4), jnp.tile(p["qk_norm"][l, 1], 4)]).astype(F32)
    proj = _in_proj(x, p["norm_attn"][l][None, :], w_in, tables, gmat, qk_gain)

    lv = p["diff_lambda"][l].astype(F32)
    lam = (jnp.exp(jnp.sum(lv[0] * lv[1])) - jnp.exp(jnp.sum(lv[2] * lv[3])) + lam_init).reshape(1)
    tq = min(TQ_FULL, S)
    sub_gain = jnp.broadcast_to(p["diff_subln"][l].astype(F32)[:, None], (HEAD_DIM, tq))
    o_a = _full_attention(proj, lam, sub_gain, q_cb=CB_AQ, k_cb=CB_AK, v_cb=CB_AV, heads=_HEADS_A,
                          n_kv=4, v_row0=0, unroll=2, diff=True, post_scale=1.0 - lam_init)
    o_d = _full_attention(proj, lam, sub_gain, q_cb=CB_DQ, k_cb=CB_DKV, v_cb=CB_DKV, heads=_HEADS_D,
                          n_kv=2, v_row0=2 * HEAD_DIM, unroll=4, diff=False, post_scale=1.0)
    o_b = _neighbourhood(proj, _nbr_bias(p["na_rpb"][l], S))
    oc, lse = zip(*[_dilated_pattern(proj, dil) for _, dil in C_PATTERNS])
    x = _merge(x, proj, o_a, o_b, o_d, oc, lse, p["w_branch"][l].astype(BF16), p["w_out"][l].astype(BF16))
    return _mlp(x, p["norm_mlp"][l][None, :], p["w_up"][l].astype(BF16), p["conv_w"][l],
                p["conv_b"][l][None, :], p["w_down"][l].astype(BF16), p["norm_final"][None, :],
                final_norm=(l == DEPTH - 1))


def _trunk(x, tables, gmat, p):
    for l in range(DEPTH):
        x = _layer(x, l, tables, gmat, p)
    return x


def _group_mean_matrix():
    head = np.arange(BRANCH_W) // HEAD_DIM
    return jnp.asarray((head[:, None] == head[None, :]) / HEAD_DIM, dtype=BF16)


def kernel(x_prompt, x_sample, norm_attn, w_in, diff_lambda, diff_subln, na_rpb, qk_norm, w_branch,
           w_out, norm_mlp, w_up, conv_w, conv_b, w_down, norm_final):
    p = dict(norm_attn=norm_attn, w_in=w_in, diff_lambda=diff_lambda, diff_subln=diff_subln,
             na_rpb=na_rpb, qk_norm=qk_norm, w_branch=w_branch, w_out=w_out, norm_mlp=norm_mlp,
             w_up=w_up, conv_w=conv_w, conv_b=conv_b, w_down=w_down, norm_final=norm_final)
    gmat = _group_mean_matrix()
    outs = []
    for x in (x_prompt, x_sample):
        tables = _rope_tables(x.shape[1])
        outs.append(_trunk(x, tables, gmat, p))
    return tuple(outs)
```

```python
import functools
import math

import jax
import jax.numpy as jnp
import numpy as np
from jax import lax
from jax.experimental import pallas as pl
from jax.experimental.pallas import tpu as pltpu

F32 = jnp.float32
BF16 = jnp.bfloat16

D_MODEL = 1024
HEAD_DIM = 64
LANES = 128
DIFF_DIM = 32
BRANCH_W = 256
N_BRANCH = 4
GRID_W = 64
NA_ROWS = 8
NA_COLS = 16
C_PATTERNS = ((128, 1), (512, 4), (2048, 16))
ROPE_THETA = 500000.0
AXIAL_THETA = 10000.0
D_FF = 2816
EPS = 1e-6
NEG_INF = -1e30
DEPTH = 2
IN_COLS = 6912
N_COL_BLOCKS = IN_COLS // BRANCH_W
N_GATE_BLOCKS = N_BRANCH * D_MODEL // BRANCH_W
N_MIX_COLS = IN_COLS - N_BRANCH * D_MODEL
LOG2E = 1.4426950408889634

(CB_AQ, CB_AK, CB_AV, CB_BQ, CB_BK, CB_BV, CB_CQ, CB_CK, CB_CV, CB_DQ,
 CB_DKV) = range(N_GATE_BLOCKS, N_GATE_BLOCKS + 11)

VMEM_LIMIT = 56 * 1024 * 1024

TM_PROJ = 1024
PROJ_ROW_CHUNK = 512
TQ_FULL = 512
TK_FULL = 512
ONES_ROWS = 16
QK_LOOKAHEAD = 3
SCORE_SLOTS = QK_LOOKAHEAD + 1
TQ_WIN = 512
WIN_SUB = 128
NBR_BLOCKS_PER_STEP = 4
C_HALF = 64
B_QROWS = 2
B_KROWS = 10
TM_MERGE = 512
TM_MLP = 1024
MLP_ROW_CHUNK = 512
MLP_HALO = 16
FF_CHUNK = 256


def _params(*sem):
    return pltpu.CompilerParams(dimension_semantics=sem, vmem_limit_bytes=VMEM_LIMIT)


def _rotate_pairs(y, cos, sin, half, group):
    lane = lax.broadcasted_iota(jnp.int32, y.shape, 1) % group
    second = (lane >= half) & (lane < 2 * half)
    from_below = pltpu.roll(y, half, 1)
    from_above = pltpu.roll(y, y.shape[1] - half, 1)
    return y * cos + jnp.where(second, from_below, from_above) * sin


def _head_rms(y, gmat, gain):
    ms = jnp.dot((y * y).astype(BF16), gmat, preferred_element_type=F32)
    return y * lax.rsqrt(ms + EPS) * gain


def _in_proj_kernel(x_ref, g_ref, w_ref, tab_ref, gmat_ref, qkg_ref, o_ref, c4_ref, c16_ref,
                    h_ref, stage_ref):
    j = pl.program_id(2)

    @pl.when(j == 0)
    def _():
        x = x_ref[0]
        ms = jnp.mean(x * x, axis=-1, keepdims=True)
        h_ref[...] = ((x * lax.rsqrt(ms + EPS)) * g_ref[...]).astype(BF16)

    scale_a = DIFF_DIM ** -0.5 * LOG2E
    scale_d = HEAD_DIM ** -0.5 * LOG2E
    scale_w = HEAD_DIM ** -0.5
    tm = h_ref.shape[0]
    rc = min(PROJ_ROW_CHUNK, tm)

    def project(epilogue, regroup=False):
        for c in range(tm // rc):
            rows = slice(c * rc, (c + 1) * rc)
            acc = jnp.dot(h_ref[rows, :], w_ref[...], preferred_element_type=F32)
            y = epilogue(acc, rows)
            o_ref[0, rows, :] = y.astype(BF16)
            if regroup:
                for half in range(2):
                    stage_ref[half, rows, :] = y[:, half * LANES:(half + 1) * LANES]
        if regroup:
            for dil, ref in ((4, c4_ref), (16, c16_ref)):
                for r in range(dil):
                    for half in range(2):
                        part = stage_ref[half, pl.ds(r, tm // dil, stride=dil), :]
                        ref[0, r, :, half * LANES:(half + 1) * LANES] = part.astype(BF16)

    def rope_a(y, rows):
        return _rotate_pairs(y, tab_ref[0, rows, :], tab_ref[1, rows, :], 4, 32)

    def rope_c(y, rows):
        return _rotate_pairs(y, tab_ref[2, rows, :], tab_ref[3, rows, :], 8, 64)

    def rope_d(y, rows):
        return _rotate_pairs(y, tab_ref[4, rows, :], tab_ref[5, rows, :], 16, 32)

    @pl.when(j == CB_AQ)
    def _():
        project(lambda acc, rows: rope_a(acc, rows) * scale_a)

    @pl.when(j == CB_AK)
    def _():
        project(rope_a)

    @pl.when((j == CB_AV) | (j == CB_BK) | (j == CB_BV))
    def _():
        project(lambda acc, rows: acc)

    @pl.when(j == CB_BQ)
    def _():
        project(lambda acc, rows: acc * scale_w)

    @pl.when(j == CB_CQ)
    def _():
        project(lambda acc, rows: rope_c(acc, rows) * scale_w, regroup=True)

    @pl.when(j == CB_CK)
    def _():
        project(rope_c, regroup=True)

    @pl.when(j == CB_CV)
    def _():
        project(lambda acc, rows: acc, regroup=True)

    @pl.when(j == CB_DQ)
    def _():
        project(lambda acc, rows:
                rope_d(_head_rms(acc, gmat_ref[...], qkg_ref[0:1, :]), rows) * scale_d)

    @pl.when(j == CB_DKV)
    def _():
        def keys_and_values(acc, rows):
            lane = lax.broadcasted_iota(jnp.int32, acc.shape, 1)
            keys = rope_d(_head_rms(acc, gmat_ref[...], qkg_ref[1:2, :]), rows)
            return jnp.where(lane < 2 * HEAD_DIM, keys, acc)

        project(keys_and_values)

    @pl.when(j < N_GATE_BLOCKS)
    def _():
        project(lambda acc, rows: 0.5 + 0.5 * jnp.tanh(0.5 * acc))


def _in_proj(x, gain, w_bf16, tables, gmat, qk_gain):
    B, S, _ = x.shape
    tm = min(TM_PROJ, S)
    grid = (S // tm, B, N_COL_BLOCKS)

    def regrouped(dil):
        return pl.BlockSpec((1, dil, tm // dil, BRANCH_W),
                            lambda i, b, j: (b, 0, i, jnp.clip(j - CB_CQ, 0, 2)))

    return pl.pallas_call(
        _in_proj_kernel,
        grid=grid,
        in_specs=[
            pl.BlockSpec((1, tm, D_MODEL), lambda i, b, j: (b, i, 0)),
            pl.BlockSpec((1, D_MODEL), lambda i, b, j: (0, 0)),
            pl.BlockSpec((D_MODEL, BRANCH_W), lambda i, b, j: (0, j)),
            pl.BlockSpec((6, tm, BRANCH_W), lambda i, b, j: (0, i, 0)),
            pl.BlockSpec((BRANCH_W, BRANCH_W), lambda i, b, j: (0, 0)),
            pl.BlockSpec((2, BRANCH_W), lambda i, b, j: (0, 0)),
        ],
        out_specs=[pl.BlockSpec((1, tm, BRANCH_W), lambda i, b, j: (b, i, j)),
                   regrouped(4), regrouped(16)],
        out_shape=[jax.ShapeDtypeStruct((B, S, IN_COLS), BF16),
                   jax.ShapeDtypeStruct((B, 4, S // 4, 3 * BRANCH_W), BF16),
                   jax.ShapeDtypeStruct((B, 16, S // 16, 3 * BRANCH_W), BF16)],
        scratch_shapes=[pltpu.VMEM((tm, D_MODEL), BF16),
                        pltpu.VMEM((BRANCH_W // LANES, tm, LANES), F32)],
        compiler_params=_params("arbitrary", "arbitrary", "arbitrary"),
        name="in_proj",
    )(x, gain, w_bf16, tables, gmat, qk_gain)


def _rope_tables(S):
    pos = jnp.arange(S, dtype=jnp.int32)
    lane = np.arange(BRANCH_W)

    def build(group, half, ang_of_lane):
        m = lane % group
        first = m < half
        second = (m >= half) & (m < 2 * half)
        ang = ang_of_lane
        cos = jnp.where(jnp.asarray(first | second)[None, :], jnp.cos(ang), 1.0)
        sin = jnp.where(jnp.asarray(second)[None, :], jnp.sin(ang),
                        jnp.where(jnp.asarray(first)[None, :], -jnp.sin(ang), 0.0))
        return cos.astype(F32), sin.astype(F32)

    def angles(p, theta, half, idx):
        inv = jnp.exp(-math.log(theta) * jnp.arange(half, dtype=F32) / half)
        ang = p.astype(F32)[:, None] * inv[None, :]
        return ang[:, idx]

    ca, sa = build(32, 4, angles(pos, ROPE_THETA, 4, (lane % 32) % 4))
    cc, sc = build(64, 8, angles(pos, ROPE_THETA, 8, (lane % 64) % 8))
    idx = (lane % 32) % 16
    ang_row = angles(pos // GRID_W, AXIAL_THETA, 16, idx)
    ang_col = angles(pos % GRID_W, AXIAL_THETA, 16, idx)
    ang_d = jnp.where(jnp.asarray((lane % 64) < 32)[None, :], ang_row, ang_col)
    cd, sd = build(32, 16, ang_d)
    return jnp.stack([ca, sa, cc, sc, cd, sd], axis=0)


def _full_attn_kernel(lam_ref, q_ref, k_ref, v_ref, sub_ref, o_ref, vt_ref, qpad_ref, m_ref, acc_ref,
                      ot_ref, s_ref, *, heads, n_kv, v_row0, unroll, diff, post_scale):
    S = k_ref.shape[1]
    tq = q_ref.shape[1]
    tk = min(TK_FULL, S)
    n_kt = S // tk
    n_vh = len(heads)
    vrows = vt_ref.shape[1]

    @pl.when(pl.program_id(1) == 0)
    def _():
        chunk = min(512, S)

        def tr(c, carry):
            r0 = pl.multiple_of(c * chunk, chunk)
            vct = v_ref[0, pl.ds(r0, chunk), :].astype(F32).T
            for g in range(n_kv):
                lo = v_row0 + HEAD_DIM * g
                vt_ref[g, 0:HEAD_DIM, pl.ds(r0, chunk)] = vct[lo:lo + HEAD_DIM, :].astype(BF16)
                vt_ref[g, HEAD_DIM:vrows, pl.ds(r0, chunk)] = jnp.ones((vrows - HEAD_DIM, chunk), BF16)
            return carry

        lax.fori_loop(0, S // chunk, tr, 0)

    qt = q_ref[0].astype(F32).T.astype(BF16)
    for vh, (slo, shi, dlo, _) in enumerate(heads):
        qpad_ref[vh] = jnp.zeros((BRANCH_W, tq), BF16)
        qpad_ref[vh, dlo:dlo + (shi - slo), :] = qt[slo:shi, :]
    m_ref[...] = jnp.full(m_ref.shape, NEG_INF, F32)
    acc_ref[...] = jnp.zeros(acc_ref.shape, F32)

    n_steps = unroll * n_vh
    n_slot = s_ref.shape[0]
    assert n_steps % n_slot == 0 and QK_LOOKAHEAD < n_slot and n_kt % unroll == 0

    def key_rows(it, step):
        kt = jnp.minimum(it * unroll + step // n_vh, n_kt - 1)
        return pl.multiple_of(kt * tk, tk)

    def scores(it, step):
        s_ref[step % n_slot] = jnp.dot(k_ref[0, pl.ds(key_rows(it, step), tk), :],
                                       qpad_ref[step % n_vh], preferred_element_type=F32)

    for step in range(QK_LOOKAHEAD):
        scores(0, step)

    def body(it, carry):
        for step in range(n_steps):
            vh = step % n_vh
            scores(it, step + QK_LOOKAHEAD)
            s = s_ref[step % n_slot]
            m_old = m_ref[vh]
            m_new = jnp.maximum(m_old, jnp.max(s, axis=0, keepdims=True))
            alpha = jnp.exp2(m_old - m_new)
            p = jnp.exp2(s - m_new).astype(BF16)
            vt = vt_ref[heads[vh][3], :, pl.ds(key_rows(it, step), tk)]
            acc_ref[vh] = alpha * acc_ref[vh] + jnp.dot(vt, p, preferred_element_type=F32)
            m_ref[vh] = m_new
        return carry

    lax.fori_loop(0, n_kt // unroll, body, 0)

    def head_out(vh):
        a = acc_ref[vh]
        return a[0:HEAD_DIM, :] / a[HEAD_DIM:HEAD_DIM + 1, :]

    if diff:
        lam = lam_ref[0]
        for h in range(n_vh // 2):
            o = head_out(2 * h) - lam * head_out(2 * h + 1)
            ms = jnp.mean(o * o, axis=0, keepdims=True)
            y = (o * lax.rsqrt(ms + EPS)) * sub_ref[...]
            ot_ref[HEAD_DIM * h:HEAD_DIM * (h + 1), :] = y * post_scale
    else:
        for h in range(n_vh):
            ot_ref[HEAD_DIM * h:HEAD_DIM * (h + 1), :] = head_out(h)
    o_ref[0] = ot_ref[...].T.astype(BF16)


def _full_attention(proj, lam, sub_gain, *, q_cb, k_cb, v_cb, heads, n_kv, v_row0, unroll, diff,
                    post_scale):
    B, S, _ = proj.shape
    tq = min(TQ_FULL, S)
    kern = functools.partial(_full_attn_kernel, heads=heads, n_kv=n_kv, v_row0=v_row0,
                             unroll=unroll, diff=diff, post_scale=post_scale)
    return pl.pallas_call(
        kern,
        grid=(B, S // tq),
        in_specs=[
            pl.BlockSpec(memory_space=pltpu.SMEM),
            pl.BlockSpec((1, tq, BRANCH_W), lambda b, i: (b, i, q_cb)),
            pl.BlockSpec((1, S, BRANCH_W), lambda b, i: (b, 0, k_cb)),
            pl.BlockSpec((1, S, BRANCH_W), lambda b, i: (b, 0, v_cb)),
            pl.BlockSpec((HEAD_DIM, tq), lambda b, i: (0, 0)),
        ],
        out_specs=pl.BlockSpec((1, tq, BRANCH_W), lambda b, i: (b, i, 0)),
        out_shape=jax.ShapeDtypeStruct((B, S, BRANCH_W), BF16),
        scratch_shapes=[
            pltpu.VMEM((n_kv, HEAD_DIM + ONES_ROWS, S), BF16),
            pltpu.VMEM((len(heads), BRANCH_W, tq), BF16),
            pltpu.VMEM((len(heads), 1, tq), F32),
            pltpu.VMEM((len(heads), HEAD_DIM + ONES_ROWS, tq), F32),
            pltpu.VMEM((BRANCH_W, tq), F32),
            pltpu.VMEM((SCORE_SLOTS, min(TK_FULL, S), tq), F32),
        ],
        compiler_params=_params("arbitrary", "arbitrary"),
        name="full_attn_diff" if diff else "full_attn_gqa",
    )(lam, proj, proj, proj, sub_gain)


_HEADS_A = tuple((HEAD_DIM * h + DIFF_DIM * c, HEAD_DIM * h + DIFF_DIM * (c + 1),
                  HEAD_DIM * h + DIFF_DIM * c, h) for h in range(4) for c in range(2))
_HEADS_D = tuple((HEAD_DIM * h, HEAD_DIM * (h + 1), HEAD_DIM * (h // 2), h // 2) for h in range(4))


def _attend_blocks(blocks, want_lse):
    tq = blocks[0][0].shape[0]
    lane = lax.broadcasted_iota(jnp.int32, (tq, BRANCH_W), 1)
    in_head = [(lane >= HEAD_DIM * h) & (lane < HEAD_DIM * (h + 1)) for h in range(4)]
    scores = [[lax.dot_general(jnp.where(in_head[h], q, jnp.zeros_like(q)), kw,
                               (((1,), (1,)), ((), ())), preferred_element_type=F32)
               for h in range(4)] for q, kw, _, _ in blocks]
    stats = []
    for (_, _, _, bias_of_head), s_heads in zip(blocks, scores):
        per_head = []
        for h in range(4):
            s = bias_of_head(h, s_heads[h])
            m = jnp.max(s, axis=-1, keepdims=True)
            p = jnp.exp(s - m)
            per_head.append((m, jnp.sum(p, axis=-1, keepdims=True), p.astype(BF16)))
        stats.append(per_head)
    outs = []
    for (_, _, vw, _), per_head in zip(blocks, stats):
        o = jnp.zeros((tq, BRANCH_W), F32)
        lse = jnp.zeros((tq, BRANCH_W), F32)
        for h in range(4):
            m, l, p = per_head[h]
            of = jnp.dot(p, vw, preferred_element_type=F32)
            o = jnp.where(in_head[h], of / l, o)
            if want_lse:
                lse = jnp.where(in_head[h], m + jnp.log(l), lse)
        outs.append((o, lse))
    return outs


def _band_kernel(q_ref, k_ref, v_ref, o_ref, lse_ref):
    L = k_ref.shape[1]
    tq = q_ref.shape[1]
    sub = min(WIN_SUB, tq)
    kwin = min(sub + 2 * C_HALF, L)
    blocks = []
    for j in range(tq // sub):
        q0 = pl.program_id(2) * tq + j * sub
        ks = pl.multiple_of(jnp.clip(q0 - C_HALF, 0, L - kwin), C_HALF)
        qpos = q0 + lax.broadcasted_iota(jnp.int32, (sub, kwin), 0)
        kpos = ks + lax.broadcasted_iota(jnp.int32, (sub, kwin), 1)
        valid = jnp.abs(qpos - kpos) <= C_HALF
        blocks.append((q_ref[0, j * sub:(j + 1) * sub, :], k_ref[0, pl.ds(ks, kwin), :],
                       v_ref[0, pl.ds(ks, kwin), :],
                       lambda h, s, valid=valid: jnp.where(valid, s, NEG_INF)))
    for j, (o, lse) in enumerate(_attend_blocks(blocks, True)):
        o_ref[0, j * sub:(j + 1) * sub, :] = o.astype(BF16)
        lse_ref[0, j * sub:(j + 1) * sub, :] = lse


def _dilated_pattern(qkv, dil, cb0):
    B, _, L, _ = qkv.shape
    tq = min(TQ_WIN, L)

    def spec(rows, col):
        return pl.BlockSpec((None, 1, rows, BRANCH_W),
                            lambda b, r, i: (b, r, i if rows == tq else 0, col))

    return pl.pallas_call(
        _band_kernel,
        grid=(B, dil, L // tq),
        in_specs=[spec(tq, cb0), spec(L, cb0 + 1), spec(L, cb0 + 2)],
        out_specs=[spec(tq, 0), spec(tq, 0)],
        out_shape=[
            jax.ShapeDtypeStruct((B, dil, L, BRANCH_W), BF16),
            jax.ShapeDtypeStruct((B, dil, L, BRANCH_W), F32),
        ],
        compiler_params=_params("arbitrary", "arbitrary", "arbitrary"),
        name=f"dilated_d{dil}",
    )(qkv, qkv, qkv)


def _nbr_kernel(q_ref, k_ref, v_ref, bias_ref, o_ref):
    S = k_ref.shape[1]
    rows = S // GRID_W
    sub = B_QROWS * GRID_W
    n_sub = q_ref.shape[1] // sub
    n_steps = rows // B_QROWS
    blocks = []
    for j in range(n_sub):
        t = pl.program_id(1) * n_sub + j
        ks_row = jnp.clip(B_QROWS * t - NA_ROWS // 2, 0, rows - B_KROWS)
        ks = pl.multiple_of(ks_row * GRID_W, GRID_W)
        variant = _nbr_variant(t, n_steps)
        blocks.append((q_ref[0, j * sub:(j + 1) * sub, :],
                       k_ref[0, pl.ds(ks, B_KROWS * GRID_W), :],
                       v_ref[0, pl.ds(ks, B_KROWS * GRID_W), :],
                       lambda h, s, variant=variant: s + bias_ref[variant, h]))
    for j, (o, _) in enumerate(_attend_blocks(blocks, False)):
        o_ref[0, j * sub:(j + 1) * sub, :] = o.astype(BF16)


def _nbr_variant(t, n_steps):
    return jnp.where(t < 2, t, jnp.where(t >= n_steps - 2, t - (n_steps - 5), 2))


def _nbr_bias(rpb, S):
    rows = S // GRID_W
    n_steps = rows // B_QROWS
    steps = np.array([0, 1, 2, n_steps - 2, n_steps - 1])
    r = (steps[:, None] * B_QROWS + np.arange(B_QROWS)[None, :])
    ks_row = np.clip(steps * B_QROWS - NA_ROWS // 2, 0, rows - B_KROWS)
    kabs = ks_row[:, None] + np.arange(B_KROWS)[None, :]
    rs = np.clip(r - NA_ROWS // 2, 0, rows - NA_ROWS)
    row_ok = (kabs[:, None, :] >= rs[:, :, None]) & (kabs[:, None, :] < rs[:, :, None] + NA_ROWS)
    dr = np.clip(kabs[:, None, :] - r[:, :, None] + NA_ROWS - 1, 0, 2 * NA_ROWS - 2)
    c = np.arange(GRID_W)
    col_start = np.clip(c - NA_COLS // 2, 0, GRID_W - NA_COLS)
    col_ok = (c[None, :] >= col_start[:, None]) & (c[None, :] < col_start[:, None] + NA_COLS)
    dc = np.clip(c[None, :] - c[:, None] + NA_COLS - 1, 0, 2 * NA_COLS - 2)
    H = rpb.shape[0]
    onehot = (dc.reshape(-1)[None, :] == np.arange(2 * NA_COLS - 1)[:, None]).astype(np.float32)
    by_col = jnp.einsum("hdc,cn->hdn", rpb.astype(F32), jnp.asarray(onehot),
                        precision=lax.Precision.HIGHEST).reshape(H, 2 * NA_ROWS - 1, GRID_W, GRID_W)
    vals = jnp.stack([by_col[:, int(d)] for d in dr.reshape(-1)], axis=1)
    vals = vals.reshape(H, 5, B_QROWS, B_KROWS, GRID_W, GRID_W)
    vals = vals.transpose(0, 1, 2, 4, 3, 5)
    OK = row_ok[:, :, None, :, None] & col_ok[None, None, :, None, :]
    vals = jnp.where(jnp.asarray(OK)[None], vals, NEG_INF)
    return vals.reshape(H, 5, B_QROWS * GRID_W, B_KROWS * GRID_W).transpose(1, 0, 2, 3)


def _neighbourhood(proj, bias):
    B, S, _ = proj.shape
    sub = B_QROWS * GRID_W
    tq = min(NBR_BLOCKS_PER_STEP * sub, S)
    return pl.pallas_call(
        _nbr_kernel,
        grid=(B, S // tq),
        in_specs=[
            pl.BlockSpec((1, tq, BRANCH_W), lambda b, t: (b, t, CB_BQ)),
            pl.BlockSpec((1, S, BRANCH_W), lambda b, t: (b, 0, CB_BK)),
            pl.BlockSpec((1, S, BRANCH_W), lambda b, t: (b, 0, CB_BV)),
            pl.BlockSpec((5, 4, sub, B_KROWS * GRID_W), lambda b, t: (0, 0, 0, 0)),
        ],
        out_specs=pl.BlockSpec((1, tq, BRANCH_W), lambda b, t: (b, t, 0)),
        out_shape=jax.ShapeDtypeStruct((B, S, BRANCH_W), BF16),
        compiler_params=_params("arbitrary", "arbitrary"),
        name="neighbourhood",
    )(proj, proj, proj, bias)


def _merge_kernel(x_ref, oa_ref, ob_ref, od_ref, oc1_ref, oc2_ref, oc3_ref, l1_ref, l2_ref, l3_ref,
                  g0_ref, g1_ref, g2_ref, g3_ref, wb_ref, wo_ref, y_ref, il_ref):
    tm = x_ref.shape[1]

    def natural(ref):
        dil = ref.shape[0]
        if dil == 1:
            return ref[0].astype(F32)
        for r in range(dil):
            v = ref[r].astype(F32)
            for half in range(2):
                il_ref[half, pl.ds(r, tm // dil, stride=dil), :] = v[:, half * LANES:(half + 1) * LANES]
        return jnp.concatenate([il_ref[0], il_ref[1]], axis=1)

    l1, l2, l3 = natural(l1_ref), natural(l2_ref), natural(l3_ref)
    m = jnp.maximum(jnp.maximum(l1, l2), l3)
    e1, e2, e3 = jnp.exp(l1 - m), jnp.exp(l2 - m), jnp.exp(l3 - m)
    oc = (e1 * natural(oc1_ref) + e2 * natural(oc2_ref) + e3 * natural(oc3_ref)) / (e1 + e2 + e3)

    def branch(o, g_ref, i):
        return g_ref[0].astype(F32) * jnp.dot(o, wb_ref[i], preferred_element_type=F32)

    merged = branch(oa_ref[0], g0_ref, 0)
    merged = merged + branch(ob_ref[0], g1_ref, 1)
    merged = merged + branch(oc.astype(BF16), g2_ref, 2)
    merged = merged + branch(od_ref[0], g3_ref, 3)
    y_ref[0] = x_ref[0] + jnp.dot(merged.astype(BF16), wo_ref[...], preferred_element_type=F32)


def _merge(x, proj, o_a, o_b, o_d, oc, lse, w_branch, w_out):
    B, S, _ = x.shape
    tm = min(TM_MERGE, S)
    tok256 = pl.BlockSpec((1, tm, BRANCH_W), lambda b, i: (b, i, 0))

    def gate_spec(g):
        return pl.BlockSpec((1, tm, D_MODEL), lambda b, i: (b, i, g))

    def grouped(arr):
        dil = arr.shape[1]
        return pl.BlockSpec((None, dil, tm // dil, BRANCH_W), lambda b, i: (b, 0, i, 0))

    return pl.pallas_call(
        _merge_kernel,
        grid=(B, S // tm),
        in_specs=[pl.BlockSpec((1, tm, D_MODEL), lambda b, i: (b, i, 0)),
                  tok256, tok256, tok256,
                  grouped(oc[0]), grouped(oc[1]), grouped(oc[2]),
                  grouped(lse[0]), grouped(lse[1]), grouped(lse[2]),
                  gate_spec(0), gate_spec(1), gate_spec(2), gate_spec(3),
                  pl.BlockSpec((N_BRANCH, BRANCH_W, D_MODEL), lambda b, i: (0, 0, 0)),
                  pl.BlockSpec((D_MODEL, D_MODEL), lambda b, i: (0, 0))],
        out_specs=pl.BlockSpec((1, tm, D_MODEL), lambda b, i: (b, i, 0)),
        out_shape=jax.ShapeDtypeStruct((B, S, D_MODEL), F32),
        scratch_shapes=[pltpu.VMEM((BRANCH_W // LANES, tm, LANES), F32)],
        compiler_params=_params("arbitrary", "arbitrary"),
        name="merge_out",
    )(x, o_a, o_b, o_d, oc[0], oc[1], oc[2], lse[0], lse[1], lse[2],
      proj, proj, proj, proj, w_branch, w_out)


def _mlp_kernel(x_ref, xp_ref, xn_ref, g_ref, wu_ref, cw_ref, cb_ref, wd_ref, gf_ref, y_ref,
                hext_ref, u_ref, act_ref, *, final_norm):
    i = pl.program_id(1)
    tm = x_ref.shape[1]
    halo = MLP_HALO
    rc = min(MLP_ROW_CHUNK, tm)
    ext = rc + 2 * halo
    c = FF_CHUNK
    n_f = D_FF // c

    def norm(x, gain):
        ms = jnp.mean(x * x, axis=-1, keepdims=True)
        return (x * lax.rsqrt(ms + EPS)) * gain

    hext_ref[0:halo, :] = jnp.where(i > 0, norm(xp_ref[0], g_ref[...]), 0.0).astype(BF16)
    hext_ref[halo:halo + tm, :] = norm(x_ref[0], g_ref[...]).astype(BF16)
    hext_ref[halo + tm:, :] = jnp.where(i < pl.num_programs(1) - 1,
                                        norm(xn_ref[0], g_ref[...]), 0.0).astype(BF16)

    def up(r0, f):
        hx = hext_ref[pl.ds(r0, ext), :]
        for half in range(2):
            lo = half * D_FF + f * c
            u_ref[f % 2, half] = jnp.dot(hx, wu_ref[:, lo:lo + c], preferred_element_type=F32)

    def conv(u, lo):
        before = pltpu.roll(u, 1, 0)[halo:halo + rc, :]
        after = pltpu.roll(u, ext - 1, 0)[halo:halo + rc, :]
        return (before * cw_ref[0:1, lo:lo + c] + u[halo:halo + rc, :] * cw_ref[1:2, lo:lo + c]
                + after * cw_ref[2:3, lo:lo + c] + cb_ref[:, lo:lo + c])

    def row_chunk(ci, carry):
        r0 = pl.multiple_of(ci * rc, rc)
        up(r0, 0)
        for f in range(n_f):
            if f + 1 < n_f:
                up(r0, f + 1)
            val = conv(u_ref[f % 2, 0], f * c)
            gt = conv(u_ref[f % 2, 1], D_FF + f * c)
            act = (0.5 * gt * (1.0 + lax.erf(gt * math.sqrt(0.5)))) * val
            act_ref[:, f * c:(f + 1) * c] = act.astype(BF16)
        y = x_ref[0, pl.ds(r0, rc), :] + jnp.dot(act_ref[...], wd_ref[...],
                                                 preferred_element_type=F32)
        if final_norm:
            y = norm(y, gf_ref[...])
        y_ref[0, pl.ds(r0, rc), :] = y
        return carry

    lax.fori_loop(0, tm // rc, row_chunk, 0)


def _mlp(x, gain, w_up, conv_w, conv_b, w_down, gain_final, final_norm):
    B, S, _ = x.shape
    tm = min(TM_MLP, S)
    rc = min(MLP_ROW_CHUNK, tm)
    hb = tm // MLP_HALO
    n_i = S // tm
    kern = functools.partial(_mlp_kernel, final_norm=final_norm)

    def resident(shape):
        return pl.BlockSpec(shape, lambda b, i: (0,) * len(shape), pipeline_mode=pl.Buffered(1))

    return pl.pallas_call(
        kern,
        grid=(B, n_i),
        in_specs=[
            pl.BlockSpec((1, tm, D_MODEL), lambda b, i: (b, i, 0)),
            pl.BlockSpec((1, MLP_HALO, D_MODEL), lambda b, i: (b, jnp.maximum(i * hb - 1, 0), 0)),
            pl.BlockSpec((1, MLP_HALO, D_MODEL),
                         lambda b, i: (b, jnp.minimum((i + 1) * hb, S // MLP_HALO - 1), 0)),
            resident((1, D_MODEL)),
            resident((D_MODEL, 2 * D_FF)),
            resident((3, 2 * D_FF)),
            resident((1, 2 * D_FF)),
            resident((D_FF, D_MODEL)),
            resident((1, D_MODEL)),
        ],
        out_specs=pl.BlockSpec((1, tm, D_MODEL), lambda b, i: (b, i, 0)),
        out_shape=jax.ShapeDtypeStruct((B, S, D_MODEL), F32),
        scratch_shapes=[
            pltpu.VMEM((tm + 2 * MLP_HALO, D_MODEL), BF16),
            pltpu.VMEM((2, 2, rc + 2 * MLP_HALO, FF_CHUNK), F32),
            pltpu.VMEM((rc, D_FF), BF16),
        ],
        compiler_params=_params("arbitrary", "arbitrary"),
        name="mlp",
    )(x, x, x, gain, w_up, conv_w, conv_b, w_down, gain_final)


def _layer(x, l, tables, gmat, p):
    lam_init = 0.8 - 0.6 * math.exp(-0.3 * l)
    S = x.shape[1]
    w_in = p["w_in"][l]
    w_in = jnp.concatenate([w_in[:, N_MIX_COLS:], w_in[:, :N_MIX_COLS]], axis=1).astype(BF16)
    qk_gain = jnp.stack([jnp.tile(p["qk_norm"][l, 0], 4), jnp.tile(p["qk_norm"][l, 1], 4)]).astype(F32)
    proj, c_by4, c_by16 = _in_proj(x, p["norm_attn"][l][None, :], w_in, tables, gmat, qk_gain)

    lv = p["diff_lambda"][l].astype(F32)
    lam = (jnp.exp(jnp.sum(lv[0] * lv[1])) - jnp.exp(jnp.sum(lv[2] * lv[3])) + lam_init).reshape(1)
    tq = min(TQ_FULL, S)
    sub_gain = jnp.broadcast_to(p["diff_subln"][l].astype(F32)[:, None], (HEAD_DIM, tq))
    o_a = _full_attention(proj, lam, sub_gain, q_cb=CB_AQ, k_cb=CB_AK, v_cb=CB_AV, heads=_HEADS_A,
                          n_kv=4, v_row0=0, unroll=2, diff=True, post_scale=1.0 - lam_init)
    o_d = _full_attention(proj, lam, sub_gain, q_cb=CB_DQ, k_cb=CB_DKV, v_cb=CB_DKV, heads=_HEADS_D,
                          n_kv=2, v_row0=2 * HEAD_DIM, unroll=4, diff=False, post_scale=1.0)
    o_b = _neighbourhood(proj, _nbr_bias(p["na_rpb"][l], S))
    oc, lse = zip(_dilated_pattern(proj[:, None], 1, CB_CQ), _dilated_pattern(c_by4, 4, 0),
                  _dilated_pattern(c_by16, 16, 0))
    x = _merge(x, proj, o_a, o_b, o_d, oc, lse, p["w_branch"][l].astype(BF16), p["w_out"][l].astype(BF16))
    return _mlp(x, p["norm_mlp"][l][None, :], p["w_up"][l].astype(BF16), p["conv_w"][l],
                p["conv_b"][l][None, :], p["w_down"][l].astype(BF16), p["norm_final"][None, :],
                final_norm=(l == DEPTH - 1))


def _trunk(x, tables, gmat, p):
    for l in range(DEPTH):
        x = _layer(x, l, tables, gmat, p)
    return x


def _group_mean_matrix():
    head = np.arange(BRANCH_W) // HEAD_DIM
    return jnp.asarray((head[:, None] == head[None, :]) / HEAD_DIM, dtype=BF16)


def kernel(x_prompt, x_sample, norm_attn, w_in, diff_lambda, diff_subln, na_rpb, qk_norm, w_branch,
           w_out, norm_mlp, w_up, conv_w, conv_b, w_down, norm_final):
    p = dict(norm_attn=norm_attn, w_in=w_in, diff_lambda=diff_lambda, diff_subln=diff_subln,
             na_rpb=na_rpb, qk_norm=qk_norm, w_branch=w_branch, w_out=w_out, norm_mlp=norm_mlp,
             w_up=w_up, conv_w=conv_w, conv_b=conv_b, w_down=w_down, norm_final=norm_final)
    gmat = _group_mean_matrix()
    outs = []
    for x in (x_prompt, x_sample):
        tables = _rope_tables(x.shape[1])
        outs.append(_trunk(x, tables, gmat, p))
    return tuple(outs)
```

```python
import functools
import math

import jax
import jax.numpy as jnp
import numpy as np
from jax import lax
from jax.experimental import pallas as pl
from jax.experimental.pallas import tpu as pltpu

F32 = jnp.float32
BF16 = jnp.bfloat16

D_MODEL = 1024
HEAD_DIM = 64
LANES = 128
DIFF_DIM = 32
BRANCH_W = 256
N_BRANCH = 4
GRID_W = 64
NA_ROWS = 8
NA_COLS = 16
C_PATTERNS = ((128, 1), (512, 4), (2048, 16))
ROPE_THETA = 500000.0
AXIAL_THETA = 10000.0
D_FF = 2816
EPS = 1e-6
NEG_INF = -1e30
DEPTH = 2
N_MIX_COLS = 2816
N_COL_BLOCKS = N_MIX_COLS // BRANCH_W
LOG2E = 1.4426950408889634

(CB_AQ, CB_AK, CB_AV, CB_BQ, CB_BK, CB_BV, CB_CQ, CB_CK, CB_CV, CB_DQ, CB_DKV) = range(N_COL_BLOCKS)

VMEM_LIMIT = 56 * 1024 * 1024

TM_PROJ = 1024
PROJ_ROW_CHUNK = 512
TQ_FULL = 512
TK_FULL = 512
ONES_ROWS = 16
QK_LOOKAHEAD = 3
SCORE_SLOTS = QK_LOOKAHEAD + 1
TQ_WIN = 512
WIN_SUB = 128
NBR_BLOCKS_PER_STEP = 4
C_HALF = 64
B_QROWS = 2
B_KROWS = 10
TM_MERGE = 1024
MERGE_ROW_CHUNK = 256
TM_MLP = 1024
MLP_ROW_CHUNK = 512
MLP_HALO = 16
FF_CHUNK = 256


def _params(*sem):
    return pltpu.CompilerParams(dimension_semantics=sem, vmem_limit_bytes=VMEM_LIMIT)


def _rotate_pairs(y, cos, sin, half, group):
    lane = lax.broadcasted_iota(jnp.int32, y.shape, 1) % group
    second = (lane >= half) & (lane < 2 * half)
    from_below = pltpu.roll(y, half, 1)
    from_above = pltpu.roll(y, y.shape[1] - half, 1)
    return y * cos + jnp.where(second, from_below, from_above) * sin


def _head_rms(y, gmat, gain):
    ms = jnp.dot((y * y).astype(BF16), gmat, preferred_element_type=F32)
    return y * lax.rsqrt(ms + EPS) * gain


def _in_proj_kernel(x_ref, g_ref, w_ref, tab_ref, gmat_ref, qkg_ref, o_ref, c4_ref, c16_ref,
                    h_ref, stage_ref):
    j = pl.program_id(2)

    @pl.when(j == 0)
    def _():
        x = x_ref[0]
        ms = jnp.mean(x * x, axis=-1, keepdims=True)
        h_ref[...] = ((x * lax.rsqrt(ms + EPS)) * g_ref[...]).astype(BF16)

    scale_a = DIFF_DIM ** -0.5 * LOG2E
    scale_d = HEAD_DIM ** -0.5 * LOG2E
    scale_w = HEAD_DIM ** -0.5
    tm = h_ref.shape[0]
    rc = min(PROJ_ROW_CHUNK, tm)

    def project(epilogue, regroup=False):
        for c in range(tm // rc):
            rows = slice(c * rc, (c + 1) * rc)
            acc = jnp.dot(h_ref[rows, :], w_ref[...], preferred_element_type=F32)
            y = epilogue(acc, rows)
            o_ref[0, rows, :] = y.astype(BF16)
            if regroup:
                for half in range(2):
                    stage_ref[half, rows, :] = y[:, half * LANES:(half + 1) * LANES]
        if regroup:
            for dil, ref in ((4, c4_ref), (16, c16_ref)):
                for r in range(dil):
                    for half in range(2):
                        part = stage_ref[half, pl.ds(r, tm // dil, stride=dil), :]
                        ref[0, r, :, half * LANES:(half + 1) * LANES] = part.astype(BF16)

    def rope_a(y, rows):
        return _rotate_pairs(y, tab_ref[0, rows, :], tab_ref[1, rows, :], 4, 32)

    def rope_c(y, rows):
        return _rotate_pairs(y, tab_ref[2, rows, :], tab_ref[3, rows, :], 8, 64)

    def rope_d(y, rows):
        return _rotate_pairs(y, tab_ref[4, rows, :], tab_ref[5, rows, :], 16, 32)

    @pl.when(j == CB_AQ)
    def _():
        project(lambda acc, rows: rope_a(acc, rows) * scale_a)

    @pl.when(j == CB_AK)
    def _():
        project(rope_a)

    @pl.when((j == CB_AV) | (j == CB_BK) | (j == CB_BV))
    def _():
        project(lambda acc, rows: acc)

    @pl.when(j == CB_BQ)
    def _():
        project(lambda acc, rows: acc * scale_w)

    @pl.when(j == CB_CQ)
    def _():
        project(lambda acc, rows: rope_c(acc, rows) * scale_w, regroup=True)

    @pl.when(j == CB_CK)
    def _():
        project(rope_c, regroup=True)

    @pl.when(j == CB_CV)
    def _():
        project(lambda acc, rows: acc, regroup=True)

    @pl.when(j == CB_DQ)
    def _():
        project(lambda acc, rows:
                rope_d(_head_rms(acc, gmat_ref[...], qkg_ref[0:1, :]), rows) * scale_d)

    @pl.when(j == CB_DKV)
    def _():
        def keys_and_values(acc, rows):
            lane = lax.broadcasted_iota(jnp.int32, acc.shape, 1)
            keys = rope_d(_head_rms(acc, gmat_ref[...], qkg_ref[1:2, :]), rows)
            return jnp.where(lane < 2 * HEAD_DIM, keys, acc)

        project(keys_and_values)


def _in_proj(x, gain, w_bf16, tables, gmat, qk_gain):
    B, S, _ = x.shape
    tm = min(TM_PROJ, S)
    grid = (S // tm, B, N_COL_BLOCKS)

    def regrouped(dil):
        return pl.BlockSpec((1, dil, tm // dil, BRANCH_W),
                            lambda i, b, j: (b, 0, i, jnp.clip(j - CB_CQ, 0, 2)))

    return pl.pallas_call(
        _in_proj_kernel,
        grid=grid,
        in_specs=[
            pl.BlockSpec((1, tm, D_MODEL), lambda i, b, j: (b, i, 0)),
            pl.BlockSpec((1, D_MODEL), lambda i, b, j: (0, 0)),
            pl.BlockSpec((D_MODEL, BRANCH_W), lambda i, b, j: (0, j)),
            pl.BlockSpec((6, tm, BRANCH_W), lambda i, b, j: (0, i, 0)),
            pl.BlockSpec((BRANCH_W, BRANCH_W), lambda i, b, j: (0, 0)),
            pl.BlockSpec((2, BRANCH_W), lambda i, b, j: (0, 0)),
        ],
        out_specs=[pl.BlockSpec((1, tm, BRANCH_W), lambda i, b, j: (b, i, j)),
                   regrouped(4), regrouped(16)],
        out_shape=[jax.ShapeDtypeStruct((B, S, N_MIX_COLS), BF16),
                   jax.ShapeDtypeStruct((B, 4, S // 4, 3 * BRANCH_W), BF16),
                   jax.ShapeDtypeStruct((B, 16, S // 16, 3 * BRANCH_W), BF16)],
        scratch_shapes=[pltpu.VMEM((tm, D_MODEL), BF16),
                        pltpu.VMEM((BRANCH_W // LANES, tm, LANES), F32)],
        compiler_params=_params("arbitrary", "arbitrary", "arbitrary"),
        name="in_proj",
    )(x, gain, w_bf16, tables, gmat, qk_gain)


def _rope_tables(S):
    pos = jnp.arange(S, dtype=jnp.int32)
    lane = np.arange(BRANCH_W)

    def build(group, half, ang_of_lane):
        m = lane % group
        first = m < half
        second = (m >= half) & (m < 2 * half)
        ang = ang_of_lane
        cos = jnp.where(jnp.asarray(first | second)[None, :], jnp.cos(ang), 1.0)
        sin = jnp.where(jnp.asarray(second)[None, :], jnp.sin(ang),
                        jnp.where(jnp.asarray(first)[None, :], -jnp.sin(ang), 0.0))
        return cos.astype(F32), sin.astype(F32)

    def angles(p, theta, half, idx):
        inv = jnp.exp(-math.log(theta) * jnp.arange(half, dtype=F32) / half)
        ang = p.astype(F32)[:, None] * inv[None, :]
        return ang[:, idx]

    ca, sa = build(32, 4, angles(pos, ROPE_THETA, 4, (lane % 32) % 4))
    cc, sc = build(64, 8, angles(pos, ROPE_THETA, 8, (lane % 64) % 8))
    idx = (lane % 32) % 16
    ang_row = angles(pos // GRID_W, AXIAL_THETA, 16, idx)
    ang_col = angles(pos % GRID_W, AXIAL_THETA, 16, idx)
    ang_d = jnp.where(jnp.asarray((lane % 64) < 32)[None, :], ang_row, ang_col)
    cd, sd = build(32, 16, ang_d)
    return jnp.stack([ca, sa, cc, sc, cd, sd], axis=0)


def _full_attn_kernel(lam_ref, q_ref, k_ref, v_ref, sub_ref, o_ref, vt_ref, qpad_ref, m_ref, acc_ref,
                      ot_ref, s_ref, *, heads, n_kv, v_row0, unroll, diff, post_scale):
    S = k_ref.shape[1]
    tq = q_ref.shape[1]
    tk = min(TK_FULL, S)
    n_kt = S // tk
    n_vh = len(heads)
    vrows = vt_ref.shape[1]

    @pl.when(pl.program_id(1) == 0)
    def _():
        chunk = min(512, S)

        def tr(c, carry):
            r0 = pl.multiple_of(c * chunk, chunk)
            vct = v_ref[0, pl.ds(r0, chunk), :].astype(F32).T
            for g in range(n_kv):
                lo = v_row0 + HEAD_DIM * g
                vt_ref[g, 0:HEAD_DIM, pl.ds(r0, chunk)] = vct[lo:lo + HEAD_DIM, :].astype(BF16)
                vt_ref[g, HEAD_DIM:vrows, pl.ds(r0, chunk)] = jnp.ones((vrows - HEAD_DIM, chunk), BF16)
            return carry

        lax.fori_loop(0, S // chunk, tr, 0)

    qt = q_ref[0].astype(F32).T.astype(BF16)
    for vh, (slo, shi, dlo, _) in enumerate(heads):
        qpad_ref[vh] = jnp.zeros((BRANCH_W, tq), BF16)
        qpad_ref[vh, dlo:dlo + (shi - slo), :] = qt[slo:shi, :]
    m_ref[...] = jnp.full(m_ref.shape, NEG_INF, F32)
    acc_ref[...] = jnp.zeros(acc_ref.shape, F32)

    n_steps = unroll * n_vh
    n_slot = s_ref.shape[0]
    assert n_steps % n_slot == 0 and QK_LOOKAHEAD < n_slot and n_kt % unroll == 0

    def key_rows(it, step):
        kt = jnp.minimum(it * unroll + step // n_vh, n_kt - 1)
        return pl.multiple_of(kt * tk, tk)

    def scores(it, step):
        s_ref[step % n_slot] = jnp.dot(k_ref[0, pl.ds(key_rows(it, step), tk), :],
                                       qpad_ref[step % n_vh], preferred_element_type=F32)

    for step in range(QK_LOOKAHEAD):
        scores(0, step)

    def body(it, carry):
        for step in range(n_steps):
            vh = step % n_vh
            scores(it, step + QK_LOOKAHEAD)
            s = s_ref[step % n_slot]
            m_old = m_ref[vh]
            m_new = jnp.maximum(m_old, jnp.max(s, axis=0, keepdims=True))
            alpha = jnp.exp2(m_old - m_new)
            p = jnp.exp2(s - m_new).astype(BF16)
            vt = vt_ref[heads[vh][3], :, pl.ds(key_rows(it, step), tk)]
            acc_ref[vh] = alpha * acc_ref[vh] + jnp.dot(vt, p, preferred_element_type=F32)
            m_ref[vh] = m_new
        return carry

    lax.fori_loop(0, n_kt // unroll, body, 0)

    def head_out(vh):
        a = acc_ref[vh]
        return a[0:HEAD_DIM, :] / a[HEAD_DIM:HEAD_DIM + 1, :]

    if diff:
        lam = lam_ref[0]
        for h in range(n_vh // 2):
            o = head_out(2 * h) - lam * head_out(2 * h + 1)
            ms = jnp.mean(o * o, axis=0, keepdims=True)
            y = (o * lax.rsqrt(ms + EPS)) * sub_ref[...]
            ot_ref[HEAD_DIM * h:HEAD_DIM * (h + 1), :] = y * post_scale
    else:
        for h in range(n_vh):
            ot_ref[HEAD_DIM * h:HEAD_DIM * (h + 1), :] = head_out(h)
    o_ref[0] = ot_ref[...].T.astype(BF16)


def _full_attention(proj, lam, sub_gain, *, q_cb, k_cb, v_cb, heads, n_kv, v_row0, unroll, diff,
                    post_scale):
    B, S, _ = proj.shape
    tq = min(TQ_FULL, S)
    kern = functools.partial(_full_attn_kernel, heads=heads, n_kv=n_kv, v_row0=v_row0,
                             unroll=unroll, diff=diff, post_scale=post_scale)
    return pl.pallas_call(
        kern,
        grid=(B, S // tq),
        in_specs=[
            pl.BlockSpec(memory_space=pltpu.SMEM),
            pl.BlockSpec((1, tq, BRANCH_W), lambda b, i: (b, i, q_cb)),
            pl.BlockSpec((1, S, BRANCH_W), lambda b, i: (b, 0, k_cb)),
            pl.BlockSpec((1, S, BRANCH_W), lambda b, i: (b, 0, v_cb)),
            pl.BlockSpec((HEAD_DIM, tq), lambda b, i: (0, 0)),
        ],
        out_specs=pl.BlockSpec((1, tq, BRANCH_W), lambda b, i: (b, i, 0)),
        out_shape=jax.ShapeDtypeStruct((B, S, BRANCH_W), BF16),
        scratch_shapes=[
            pltpu.VMEM((n_kv, HEAD_DIM + ONES_ROWS, S), BF16),
            pltpu.VMEM((len(heads), BRANCH_W, tq), BF16),
            pltpu.VMEM((len(heads), 1, tq), F32),
            pltpu.VMEM((len(heads), HEAD_DIM + ONES_ROWS, tq), F32),
            pltpu.VMEM((BRANCH_W, tq), F32),
            pltpu.VMEM((SCORE_SLOTS, min(TK_FULL, S), tq), F32),
        ],
        compiler_params=_params("arbitrary", "arbitrary"),
        name="full_attn_diff" if diff else "full_attn_gqa",
    )(lam, proj, proj, proj, sub_gain)


_HEADS_A = tuple((HEAD_DIM * h + DIFF_DIM * c, HEAD_DIM * h + DIFF_DIM * (c + 1),
                  HEAD_DIM * h + DIFF_DIM * c, h) for h in range(4) for c in range(2))
_HEADS_D = tuple((HEAD_DIM * h, HEAD_DIM * (h + 1), HEAD_DIM * (h // 2), h // 2) for h in range(4))


def _attend_blocks(blocks, want_lse):
    tq = blocks[0][0].shape[0]
    lane = lax.broadcasted_iota(jnp.int32, (tq, BRANCH_W), 1)
    in_head = [(lane >= HEAD_DIM * h) & (lane < HEAD_DIM * (h + 1)) for h in range(4)]
    scores = [[lax.dot_general(jnp.where(in_head[h], q, jnp.zeros_like(q)), kw,
                               (((1,), (1,)), ((), ())), preferred_element_type=F32)
               for h in range(4)] for q, kw, _, _ in blocks]
    stats = []
    for (_, _, _, bias_of_head), s_heads in zip(blocks, scores):
        per_head = []
        for h in range(4):
            s = bias_of_head(h, s_heads[h])
            m = jnp.max(s, axis=-1, keepdims=True)
            p = jnp.exp(s - m)
            per_head.append((m, jnp.sum(p, axis=-1, keepdims=True), p.astype(BF16)))
        stats.append(per_head)
    outs = []
    for (_, _, vw, _), per_head in zip(blocks, stats):
        o = jnp.zeros((tq, BRANCH_W), F32)
        lse = jnp.zeros((tq, BRANCH_W), F32)
        for h in range(4):
            m, l, p = per_head[h]
            of = jnp.dot(p, vw, preferred_element_type=F32)
            o = jnp.where(in_head[h], of / l, o)
            if want_lse:
                lse = jnp.where(in_head[h], m + jnp.log(l), lse)
        outs.append((o, lse))
    return outs


def _band_kernel(q_ref, k_ref, v_ref, o_ref, lse_ref):
    L = k_ref.shape[1]
    tq = q_ref.shape[1]
    sub = min(WIN_SUB, tq)
    kwin = min(sub + 2 * C_HALF, L)
    blocks = []
    for j in range(tq // sub):
        q0 = pl.program_id(2) * tq + j * sub
        ks = pl.multiple_of(jnp.clip(q0 - C_HALF, 0, L - kwin), C_HALF)
        qpos = q0 + lax.broadcasted_iota(jnp.int32, (sub, kwin), 0)
        kpos = ks + lax.broadcasted_iota(jnp.int32, (sub, kwin), 1)
        valid = jnp.abs(qpos - kpos) <= C_HALF
        blocks.append((q_ref[0, j * sub:(j + 1) * sub, :], k_ref[0, pl.ds(ks, kwin), :],
                       v_ref[0, pl.ds(ks, kwin), :],
                       lambda h, s, valid=valid: jnp.where(valid, s, NEG_INF)))
    for j, (o, lse) in enumerate(_attend_blocks(blocks, True)):
        o_ref[0, j * sub:(j + 1) * sub, :] = o.astype(BF16)
        lse_ref[0, j * sub:(j + 1) * sub, :] = lse


def _dilated_pattern(qkv, dil, cb0):
    B, _, L, _ = qkv.shape
    tq = min(TQ_WIN, L)

    def spec(rows, col):
        return pl.BlockSpec((None, 1, rows, BRANCH_W),
                            lambda b, r, i: (b, r, i if rows == tq else 0, col))

    return pl.pallas_call(
        _band_kernel,
        grid=(B, dil, L // tq),
        in_specs=[spec(tq, cb0), spec(L, cb0 + 1), spec(L, cb0 + 2)],
        out_specs=[spec(tq, 0), spec(tq, 0)],
        out_shape=[
            jax.ShapeDtypeStruct((B, dil, L, BRANCH_W), BF16),
            jax.ShapeDtypeStruct((B, dil, L, BRANCH_W), F32),
        ],
        compiler_params=_params("arbitrary", "arbitrary", "arbitrary"),
        name=f"dilated_d{dil}",
    )(qkv, qkv, qkv)


def _nbr_kernel(q_ref, k_ref, v_ref, bias_ref, o_ref):
    S = k_ref.shape[1]
    rows = S // GRID_W
    sub = B_QROWS * GRID_W
    n_sub = q_ref.shape[1] // sub
    n_steps = rows // B_QROWS
    blocks = []
    for j in range(n_sub):
        t = pl.program_id(1) * n_sub + j
        ks_row = jnp.clip(B_QROWS * t - NA_ROWS // 2, 0, rows - B_KROWS)
        ks = pl.multiple_of(ks_row * GRID_W, GRID_W)
        variant = _nbr_variant(t, n_steps)
        blocks.append((q_ref[0, j * sub:(j + 1) * sub, :],
                       k_ref[0, pl.ds(ks, B_KROWS * GRID_W), :],
                       v_ref[0, pl.ds(ks, B_KROWS * GRID_W), :],
                       lambda h, s, variant=variant: s + bias_ref[variant, h]))
    for j, (o, _) in enumerate(_attend_blocks(blocks, False)):
        o_ref[0, j * sub:(j + 1) * sub, :] = o.astype(BF16)


def _nbr_variant(t, n_steps):
    return jnp.where(t < 2, t, jnp.where(t >= n_steps - 2, t - (n_steps - 5), 2))


def _nbr_bias(rpb, S):
    rows = S // GRID_W
    n_steps = rows // B_QROWS
    steps = np.array([0, 1, 2, n_steps - 2, n_steps - 1])
    r = (steps[:, None] * B_QROWS + np.arange(B_QROWS)[None, :])
    ks_row = np.clip(steps * B_QROWS - NA_ROWS // 2, 0, rows - B_KROWS)
    kabs = ks_row[:, None] + np.arange(B_KROWS)[None, :]
    rs = np.clip(r - NA_ROWS // 2, 0, rows - NA_ROWS)
    row_ok = (kabs[:, None, :] >= rs[:, :, None]) & (kabs[:, None, :] < rs[:, :, None] + NA_ROWS)
    dr = np.clip(kabs[:, None, :] - r[:, :, None] + NA_ROWS - 1, 0, 2 * NA_ROWS - 2)
    c = np.arange(GRID_W)
    col_start = np.clip(c - NA_COLS // 2, 0, GRID_W - NA_COLS)
    col_ok = (c[None, :] >= col_start[:, None]) & (c[None, :] < col_start[:, None] + NA_COLS)
    dc = np.clip(c[None, :] - c[:, None] + NA_COLS - 1, 0, 2 * NA_COLS - 2)
    H = rpb.shape[0]
    onehot = (dc.reshape(-1)[None, :] == np.arange(2 * NA_COLS - 1)[:, None]).astype(np.float32)
    by_col = jnp.einsum("hdc,cn->hdn", rpb.astype(F32), jnp.asarray(onehot),
                        precision=lax.Precision.HIGHEST).reshape(H, 2 * NA_ROWS - 1, GRID_W, GRID_W)
    vals = jnp.stack([by_col[:, int(d)] for d in dr.reshape(-1)], axis=1)
    vals = vals.reshape(H, 5, B_QROWS, B_KROWS, GRID_W, GRID_W)
    vals = vals.transpose(0, 1, 2, 4, 3, 5)
    OK = row_ok[:, :, None, :, None] & col_ok[None, None, :, None, :]
    vals = jnp.where(jnp.asarray(OK)[None], vals, NEG_INF)
    return vals.reshape(H, 5, B_QROWS * GRID_W, B_KROWS * GRID_W).transpose(1, 0, 2, 3)


def _neighbourhood(proj, bias):
    B, S, _ = proj.shape
    sub = B_QROWS * GRID_W
    tq = min(NBR_BLOCKS_PER_STEP * sub, S)
    return pl.pallas_call(
        _nbr_kernel,
        grid=(B, S // tq),
        in_specs=[
            pl.BlockSpec((1, tq, BRANCH_W), lambda b, t: (b, t, CB_BQ)),
            pl.BlockSpec((1, S, BRANCH_W), lambda b, t: (b, 0, CB_BK)),
            pl.BlockSpec((1, S, BRANCH_W), lambda b, t: (b, 0, CB_BV)),
            pl.BlockSpec((5, 4, sub, B_KROWS * GRID_W), lambda b, t: (0, 0, 0, 0)),
        ],
        out_specs=pl.BlockSpec((1, tq, BRANCH_W), lambda b, t: (b, t, 0)),
        out_shape=jax.ShapeDtypeStruct((B, S, BRANCH_W), BF16),
        compiler_params=_params("arbitrary", "arbitrary"),
        name="neighbourhood",
    )(proj, proj, proj, bias)


def _merge_kernel(x_ref, gn_ref, oa_ref, ob_ref, od_ref, oc1_ref, oc2_ref, oc3_ref, l1_ref, l2_ref,
                  l3_ref, wg_ref, wb_ref, wo_ref, y_ref, il_ref):
    tm = x_ref.shape[1]
    x = x_ref[0]
    ms = jnp.mean(x * x, axis=-1, keepdims=True)
    h = ((x * lax.rsqrt(ms + EPS)) * gn_ref[...]).astype(BF16)

    def natural(ref):
        dil = ref.shape[0]
        if dil == 1:
            return ref[0].astype(F32)
        for r in range(dil):
            v = ref[r].astype(F32)
            for half in range(2):
                il_ref[half, pl.ds(r, tm // dil, stride=dil), :] = v[:, half * LANES:(half + 1) * LANES]
        return jnp.concatenate([il_ref[0], il_ref[1]], axis=1)

    l1, l2, l3 = natural(l1_ref), natural(l2_ref), natural(l3_ref)
    m = jnp.maximum(jnp.maximum(l1, l2), l3)
    e1, e2, e3 = jnp.exp(l1 - m), jnp.exp(l2 - m), jnp.exp(l3 - m)
    oc = (e1 * natural(oc1_ref) + e2 * natural(oc2_ref) + e3 * natural(oc3_ref)) / (e1 + e2 + e3)

    oc = oc.astype(BF16)
    rc = min(MERGE_ROW_CHUNK, tm)

    def gated_sum(c):
        rows = slice(c * rc, (c + 1) * rc)
        merged = None
        for i, o in enumerate((oa_ref[0, rows, :], ob_ref[0, rows, :], oc[rows, :], od_ref[0, rows, :])):
            gate = 0.5 + 0.5 * jnp.tanh(0.5 * jnp.dot(h[rows, :], wg_ref[i], preferred_element_type=F32))
            term = gate * jnp.dot(o, wb_ref[i], preferred_element_type=F32)
            merged = term if merged is None else merged + term
        return merged.astype(BF16)

    n_c = tm // rc
    pending = gated_sum(0)
    for c in range(n_c):
        ready = pending
        if c + 1 < n_c:
            pending = gated_sum(c + 1)
        rows = slice(c * rc, (c + 1) * rc)
        y_ref[0, rows, :] = x[rows, :] + jnp.dot(ready, wo_ref[...], preferred_element_type=F32)


def _merge(x, gain, o_a, o_b, o_d, oc, lse, w_gate, w_branch, w_out):
    B, S, _ = x.shape
    tm = min(TM_MERGE, S)
    tok256 = pl.BlockSpec((1, tm, BRANCH_W), lambda b, i: (b, i, 0))

    def grouped(arr):
        dil = arr.shape[1]
        return pl.BlockSpec((None, dil, tm // dil, BRANCH_W), lambda b, i: (b, 0, i, 0))

    def resident(shape):
        return pl.BlockSpec(shape, lambda b, i: (0,) * len(shape), pipeline_mode=pl.Buffered(1))

    return pl.pallas_call(
        _merge_kernel,
        grid=(B, S // tm),
        in_specs=[pl.BlockSpec((1, tm, D_MODEL), lambda b, i: (b, i, 0)),
                  resident((1, D_MODEL)),
                  tok256, tok256, tok256,
                  grouped(oc[0]), grouped(oc[1]), grouped(oc[2]),
                  grouped(lse[0]), grouped(lse[1]), grouped(lse[2]),
                  resident((N_BRANCH, D_MODEL, D_MODEL)),
                  resident((N_BRANCH, BRANCH_W, D_MODEL)),
                  resident((D_MODEL, D_MODEL))],
        out_specs=pl.BlockSpec((1, tm, D_MODEL), lambda b, i: (b, i, 0)),
        out_shape=jax.ShapeDtypeStruct((B, S, D_MODEL), F32),
        scratch_shapes=[pltpu.VMEM((BRANCH_W // LANES, tm, LANES), F32)],
        compiler_params=_params("arbitrary", "arbitrary"),
        name="merge_out",
    )(x, gain, o_a, o_b, o_d, oc[0], oc[1], oc[2], lse[0], lse[1], lse[2],
      w_gate, w_branch, w_out)


def _mlp_kernel(x_ref, xp_ref, xn_ref, g_ref, wu_ref, cw_ref, cb_ref, wd_ref, gf_ref, y_ref,
                hext_ref, u_ref, act_ref, *, final_norm):
    i = pl.program_id(1)
    tm = x_ref.shape[1]
    halo = MLP_HALO
    rc = min(MLP_ROW_CHUNK, tm)
    ext = rc + 2 * halo
    c = FF_CHUNK
    n_f = D_FF // c

    def norm(x, gain):
        ms = jnp.mean(x * x, axis=-1, keepdims=True)
        return (x * lax.rsqrt(ms + EPS)) * gain

    hext_ref[0:halo, :] = jnp.where(i > 0, norm(xp_ref[0], g_ref[...]), 0.0).astype(BF16)
    hext_ref[halo:halo + tm, :] = norm(x_ref[0], g_ref[...]).astype(BF16)
    hext_ref[halo + tm:, :] = jnp.where(i < pl.num_programs(1) - 1,
                                        norm(xn_ref[0], g_ref[...]), 0.0).astype(BF16)

    def up(r0, f):
        hx = hext_ref[pl.ds(r0, ext), :]
        for half in range(2):
            lo = half * D_FF + f * c
            u_ref[f % 2, half] = jnp.dot(hx, wu_ref[:, lo:lo + c], preferred_element_type=F32)

    def conv(u, lo):
        before = pltpu.roll(u, 1, 0)[halo:halo + rc, :]
        after = pltpu.roll(u, ext - 1, 0)[halo:halo + rc, :]
        return (before * cw_ref[0:1, lo:lo + c] + u[halo:halo + rc, :] * cw_ref[1:2, lo:lo + c]
                + after * cw_ref[2:3, lo:lo + c] + cb_ref[:, lo:lo + c])

    def row_chunk(ci, carry):
        r0 = pl.multiple_of(ci * rc, rc)
        up(r0, 0)
        for f in range(n_f):
            if f + 1 < n_f:
                up(r0, f + 1)
            val = conv(u_ref[f % 2, 0], f * c)
            gt = conv(u_ref[f % 2, 1], D_FF + f * c)
            act = (0.5 * gt * (1.0 + lax.erf(gt * math.sqrt(0.5)))) * val
            act_ref[:, f * c:(f + 1) * c] = act.astype(BF16)
        y = x_ref[0, pl.ds(r0, rc), :] + jnp.dot(act_ref[...], wd_ref[...],
                                                 preferred_element_type=F32)
        if final_norm:
            y = norm(y, gf_ref[...])
        y_ref[0, pl.ds(r0, rc), :] = y
        return carry

    lax.fori_loop(0, tm // rc, row_chunk, 0)


def _mlp(x, gain, w_up, conv_w, conv_b, w_down, gain_final, final_norm):
    B, S, _ = x.shape
    tm = min(TM_MLP, S)
    rc = min(MLP_ROW_CHUNK, tm)
    hb = tm // MLP_HALO
    n_i = S // tm
    kern = functools.partial(_mlp_kernel, final_norm=final_norm)

    def resident(shape):
        return pl.BlockSpec(shape, lambda b, i: (0,) * len(shape), pipeline_mode=pl.Buffered(1))

    return pl.pallas_call(
        kern,
        grid=(B, n_i),
        in_specs=[
            pl.BlockSpec((1, tm, D_MODEL), lambda b, i: (b, i, 0)),
            pl.BlockSpec((1, MLP_HALO, D_MODEL), lambda b, i: (b, jnp.maximum(i * hb - 1, 0), 0)),
            pl.BlockSpec((1, MLP_HALO, D_MODEL),
                         lambda b, i: (b, jnp.minimum((i + 1) * hb, S // MLP_HALO - 1), 0)),
            resident((1, D_MODEL)),
            resident((D_MODEL, 2 * D_FF)),
            resident((3, 2 * D_FF)),
            resident((1, 2 * D_FF)),
            resident((D_FF, D_MODEL)),
            resident((1, D_MODEL)),
        ],
        out_specs=pl.BlockSpec((1, tm, D_MODEL), lambda b, i: (b, i, 0)),
        out_shape=jax.ShapeDtypeStruct((B, S, D_MODEL), F32),
        scratch_shapes=[
            pltpu.VMEM((tm + 2 * MLP_HALO, D_MODEL), BF16),
            pltpu.VMEM((2, 2, rc + 2 * MLP_HALO, FF_CHUNK), F32),
            pltpu.VMEM((rc, D_FF), BF16),
        ],
        compiler_params=_params("arbitrary", "arbitrary"),
        name="mlp",
    )(x, x, x, gain, w_up, conv_w, conv_b, w_down, gain_final)


def _layer(x, l, tables, gmat, p):
    lam_init = 0.8 - 0.6 * math.exp(-0.3 * l)
    S = x.shape[1]
    w_in = p["w_in"][l][:, :N_MIX_COLS].astype(BF16)
    w_gate = p["w_in"][l][:, N_MIX_COLS:].reshape(D_MODEL, N_BRANCH, D_MODEL)
    w_gate = w_gate.transpose(1, 0, 2).astype(BF16)
    qk_gain = jnp.stack([jnp.tile(p["qk_norm"][l, 0], 4), jnp.tile(p["qk_norm"][l, 1], 4)]).astype(F32)
    proj, c_by4, c_by16 = _in_proj(x, p["norm_attn"][l][None, :], w_in, tables, gmat, qk_gain)

    lv = p["diff_lambda"][l].astype(F32)
    lam = (jnp.exp(jnp.sum(lv[0] * lv[1])) - jnp.exp(jnp.sum(lv[2] * lv[3])) + lam_init).reshape(1)
    tq = min(TQ_FULL, S)
    sub_gain = jnp.broadcast_to(p["diff_subln"][l].astype(F32)[:, None], (HEAD_DIM, tq))
    o_a = _full_attention(proj, lam, sub_gain, q_cb=CB_AQ, k_cb=CB_AK, v_cb=CB_AV, heads=_HEADS_A,
                          n_kv=4, v_row0=0, unroll=2, diff=True, post_scale=1.0 - lam_init)
    o_d = _full_attention(proj, lam, sub_gain, q_cb=CB_DQ, k_cb=CB_DKV, v_cb=CB_DKV, heads=_HEADS_D,
                          n_kv=2, v_row0=2 * HEAD_DIM, unroll=4, diff=False, post_scale=1.0)
    o_b = _neighbourhood(proj, _nbr_bias(p["na_rpb"][l], S))
    oc, lse = zip(_dilated_pattern(proj[:, None], 1, CB_CQ), _dilated_pattern(c_by4, 4, 0),
                  _dilated_pattern(c_by16, 16, 0))
    x = _merge(x, p["norm_attn"][l][None, :], o_a, o_b, o_d, oc, lse, w_gate,
               p["w_branch"][l].astype(BF16), p["w_out"][l].astype(BF16))
    return _mlp(x, p["norm_mlp"][l][None, :], p["w_up"][l].astype(BF16), p["conv_w"][l],
                p["conv_b"][l][None, :], p["w_down"][l].astype(BF16), p["norm_final"][None, :],
                final_norm=(l == DEPTH - 1))


def _trunk(x, tables, gmat, p):
    for l in range(DEPTH):
        x = _layer(x, l, tables, gmat, p)
    return x


def _group_mean_matrix():
    head = np.arange(BRANCH_W) // HEAD_DIM
    return jnp.asarray((head[:, None] == head[None, :]) / HEAD_DIM, dtype=BF16)


def kernel(x_prompt, x_sample, norm_attn, w_in, diff_lambda, diff_subln, na_rpb, qk_norm, w_branch,
           w_out, norm_mlp, w_up, conv_w, conv_b, w_down, norm_final):
    p = dict(norm_attn=norm_attn, w_in=w_in, diff_lambda=diff_lambda, diff_subln=diff_subln,
             na_rpb=na_rpb, qk_norm=qk_norm, w_branch=w_branch, w_out=w_out, norm_mlp=norm_mlp,
             w_up=w_up, conv_w=conv_w, conv_b=conv_b, w_down=w_down, norm_final=norm_final)
    gmat = _group_mean_matrix()
    outs = []
    for x in (x_prompt, x_sample):
        tables = _rope_tables(x.shape[1])
        outs.append(_trunk(x, tables, gmat, p))
    return tuple(outs)
```

```python
import functools
import math

import jax
import jax.numpy as jnp
import numpy as np
from jax import lax
from jax.experimental import pallas as pl
from jax.experimental.pallas import tpu as pltpu

F32 = jnp.float32
BF16 = jnp.bfloat16

D_MODEL = 1024
HEAD_DIM = 64
LANES = 128
DIFF_DIM = 32
BRANCH_W = 256
N_BRANCH = 4
GRID_W = 64
NA_ROWS = 8
NA_COLS = 16
C_PATTERNS = ((128, 1), (512, 4), (2048, 16))
ROPE_THETA = 500000.0
AXIAL_THETA = 10000.0
D_FF = 2816
EPS = 1e-6
NEG_INF = -1e30
DEPTH = 2
N_MIX_COLS = 2816
N_COL_BLOCKS = N_MIX_COLS // BRANCH_W
LOG2E = 1.4426950408889634

(CB_AQ, CB_AK, CB_AV, CB_BQ, CB_BK, CB_BV, CB_CQ, CB_CK, CB_CV, CB_DQ, CB_DKV) = range(N_COL_BLOCKS)

VMEM_LIMIT = 56 * 1024 * 1024

TM_PROJ = 1024
PROJ_ROW_CHUNK = 512
TQ_FULL = 512
TK_FULL = 512
ONES_ROWS = 16
QK_LOOKAHEAD = 3
SCORE_SLOTS = QK_LOOKAHEAD + 1
TQ_WIN = 512
WIN_SUB = 128
NBR_BLOCKS_PER_STEP = 4
C_HALF = 64
B_QROWS = 2
B_KROWS = 10
TM_MERGE = 1024
MERGE_ROW_CHUNK = 256
TM_MLP = 1024
MLP_ROW_CHUNK = 1024
MLP_HALO = 16
FF_CHUNK = 256


def _params(*sem):
    return pltpu.CompilerParams(dimension_semantics=sem, vmem_limit_bytes=VMEM_LIMIT)


def _rotate_pairs(y, cos, sin, half, group):
    lane = lax.broadcasted_iota(jnp.int32, y.shape, 1) % group
    second = (lane >= half) & (lane < 2 * half)
    from_below = pltpu.roll(y, half, 1)
    from_above = pltpu.roll(y, y.shape[1] - half, 1)
    return y * cos + jnp.where(second, from_below, from_above) * sin


def _head_rms(y, gmat, gain):
    ms = jnp.dot((y * y).astype(BF16), gmat, preferred_element_type=F32)
    return y * lax.rsqrt(ms + EPS) * gain


def _in_proj_kernel(x_ref, g_ref, w_ref, tab_ref, gmat_ref, qkg_ref, o_ref, c4_ref, c16_ref,
                    h_ref, stage_ref):
    j = pl.program_id(2)

    @pl.when(j == 0)
    def _():
        x = x_ref[0]
        ms = jnp.mean(x * x, axis=-1, keepdims=True)
        h_ref[...] = ((x * lax.rsqrt(ms + EPS)) * g_ref[...]).astype(BF16)

    scale_a = DIFF_DIM ** -0.5 * LOG2E
    scale_d = HEAD_DIM ** -0.5 * LOG2E
    scale_w = HEAD_DIM ** -0.5
    tm = h_ref.shape[0]
    rc = min(PROJ_ROW_CHUNK, tm)

    def project(epilogue, regroup=False):
        for c in range(tm // rc):
            rows = slice(c * rc, (c + 1) * rc)
            acc = jnp.dot(h_ref[rows, :], w_ref[...], preferred_element_type=F32)
            y = epilogue(acc, rows)
            o_ref[0, rows, :] = y.astype(BF16)
            if regroup:
                for half in range(2):
                    stage_ref[half, rows, :] = y[:, half * LANES:(half + 1) * LANES]
        if regroup:
            for dil, ref in ((4, c4_ref), (16, c16_ref)):
                for r in range(dil):
                    for half in range(2):
                        part = stage_ref[half, pl.ds(r, tm // dil, stride=dil), :]
                        ref[0, r, :, half * LANES:(half + 1) * LANES] = part.astype(BF16)

    def rope_a(y, rows):
        return _rotate_pairs(y, tab_ref[0, rows, :], tab_ref[1, rows, :], 4, 32)

    def rope_c(y, rows):
        return _rotate_pairs(y, tab_ref[2, rows, :], tab_ref[3, rows, :], 8, 64)

    def rope_d(y, rows):
        return _rotate_pairs(y, tab_ref[4, rows, :], tab_ref[5, rows, :], 16, 32)

    @pl.when(j == CB_AQ)
    def _():
        project(lambda acc, rows: rope_a(acc, rows) * scale_a)

    @pl.when(j == CB_AK)
    def _():
        project(rope_a)

    @pl.when((j == CB_AV) | (j == CB_BK) | (j == CB_BV))
    def _():
        project(lambda acc, rows: acc)

    @pl.when(j == CB_BQ)
    def _():
        project(lambda acc, rows: acc * scale_w)

    @pl.when(j == CB_CQ)
    def _():
        project(lambda acc, rows: rope_c(acc, rows) * scale_w, regroup=True)

    @pl.when(j == CB_CK)
    def _():
        project(rope_c, regroup=True)

    @pl.when(j == CB_CV)
    def _():
        project(lambda acc, rows: acc, regroup=True)

    @pl.when(j == CB_DQ)
    def _():
        project(lambda acc, rows:
                rope_d(_head_rms(acc, gmat_ref[...], qkg_ref[0:1, :]), rows) * scale_d)

    @pl.when(j == CB_DKV)
    def _():
        def keys_and_values(acc, rows):
            lane = lax.broadcasted_iota(jnp.int32, acc.shape, 1)
            keys = rope_d(_head_rms(acc, gmat_ref[...], qkg_ref[1:2, :]), rows)
            return jnp.where(lane < 2 * HEAD_DIM, keys, acc)

        project(keys_and_values)


def _in_proj(x, gain, w_bf16, tables, gmat, qk_gain):
    B, S, _ = x.shape
    tm = min(TM_PROJ, S)
    grid = (S // tm, B, N_COL_BLOCKS)

    def regrouped(dil):
        return pl.BlockSpec((1, dil, tm // dil, BRANCH_W),
                            lambda i, b, j: (b, 0, i, jnp.clip(j - CB_CQ, 0, 2)))

    return pl.pallas_call(
        _in_proj_kernel,
        grid=grid,
        in_specs=[
            pl.BlockSpec((1, tm, D_MODEL), lambda i, b, j: (b, i, 0)),
            pl.BlockSpec((1, D_MODEL), lambda i, b, j: (0, 0)),
            pl.BlockSpec((D_MODEL, BRANCH_W), lambda i, b, j: (0, j)),
            pl.BlockSpec((6, tm, BRANCH_W), lambda i, b, j: (0, i, 0)),
            pl.BlockSpec((BRANCH_W, BRANCH_W), lambda i, b, j: (0, 0)),
            pl.BlockSpec((2, BRANCH_W), lambda i, b, j: (0, 0)),
        ],
        out_specs=[pl.BlockSpec((1, tm, BRANCH_W), lambda i, b, j: (b, i, j)),
                   regrouped(4), regrouped(16)],
        out_shape=[jax.ShapeDtypeStruct((B, S, N_MIX_COLS), BF16),
                   jax.ShapeDtypeStruct((B, 4, S // 4, 3 * BRANCH_W), BF16),
                   jax.ShapeDtypeStruct((B, 16, S // 16, 3 * BRANCH_W), BF16)],
        scratch_shapes=[pltpu.VMEM((tm, D_MODEL), BF16),
                        pltpu.VMEM((BRANCH_W // LANES, tm, LANES), F32)],
        compiler_params=_params("arbitrary", "arbitrary", "arbitrary"),
        name="in_proj",
    )(x, gain, w_bf16, tables, gmat, qk_gain)


def _rope_tables(S):
    pos = jnp.arange(S, dtype=jnp.int32)
    lane = np.arange(BRANCH_W)

    def build(group, half, ang_of_lane):
        m = lane % group
        first = m < half
        second = (m >= half) & (m < 2 * half)
        ang = ang_of_lane
        cos = jnp.where(jnp.asarray(first | second)[None, :], jnp.cos(ang), 1.0)
        sin = jnp.where(jnp.asarray(second)[None, :], jnp.sin(ang),
                        jnp.where(jnp.asarray(first)[None, :], -jnp.sin(ang), 0.0))
        return cos.astype(F32), sin.astype(F32)

    def angles(p, theta, half, idx):
        inv = jnp.exp(-math.log(theta) * jnp.arange(half, dtype=F32) / half)
        ang = p.astype(F32)[:, None] * inv[None, :]
        return ang[:, idx]

    ca, sa = build(32, 4, angles(pos, ROPE_THETA, 4, (lane % 32) % 4))
    cc, sc = build(64, 8, angles(pos, ROPE_THETA, 8, (lane % 64) % 8))
    idx = (lane % 32) % 16
    ang_row = angles(pos // GRID_W, AXIAL_THETA, 16, idx)
    ang_col = angles(pos % GRID_W, AXIAL_THETA, 16, idx)
    ang_d = jnp.where(jnp.asarray((lane % 64) < 32)[None, :], ang_row, ang_col)
    cd, sd = build(32, 16, ang_d)
    return jnp.stack([ca, sa, cc, sc, cd, sd], axis=0)


def _full_attn_kernel(lam_ref, q_ref, k_ref, v_ref, sub_ref, o_ref, vt_ref, qpad_ref, m_ref, acc_ref,
                      ot_ref, s_ref, *, heads, n_kv, v_row0, unroll, diff, post_scale):
    S = k_ref.shape[1]
    tq = q_ref.shape[1]
    tk = min(TK_FULL, S)
    n_kt = S // tk
    n_vh = len(heads)
    vrows = vt_ref.shape[1]

    @pl.when(pl.program_id(1) == 0)
    def _():
        chunk = min(512, S)

        def tr(c, carry):
            r0 = pl.multiple_of(c * chunk, chunk)
            vct = v_ref[0, pl.ds(r0, chunk), :].astype(F32).T
            for g in range(n_kv):
                lo = v_row0 + HEAD_DIM * g
                vt_ref[g, 0:HEAD_DIM, pl.ds(r0, chunk)] = vct[lo:lo + HEAD_DIM, :].astype(BF16)
                vt_ref[g, HEAD_DIM:vrows, pl.ds(r0, chunk)] = jnp.ones((vrows - HEAD_DIM, chunk), BF16)
            return carry

        lax.fori_loop(0, S // chunk, tr, 0)

    qt = q_ref[0].astype(F32).T.astype(BF16)
    for vh, (slo, shi, dlo, _) in enumerate(heads):
        qpad_ref[vh] = jnp.zeros((LANES, tq), BF16)
        qpad_ref[vh, dlo % LANES:dlo % LANES + (shi - slo), :] = qt[slo:shi, :]
    m_ref[...] = jnp.full(m_ref.shape, NEG_INF, F32)
    acc_ref[...] = jnp.zeros(acc_ref.shape, F32)

    n_steps = unroll * n_vh
    n_slot = s_ref.shape[0]
    assert n_steps % n_slot == 0 and QK_LOOKAHEAD < n_slot and n_kt % unroll == 0

    def key_rows(it, step):
        kt = jnp.minimum(it * unroll + step // n_vh, n_kt - 1)
        return pl.multiple_of(kt * tk, tk)

    def scores(it, step):
        vh = step % n_vh
        lane0 = heads[vh][2] // LANES * LANES
        s_ref[step % n_slot] = jnp.dot(k_ref[0, pl.ds(key_rows(it, step), tk), lane0:lane0 + LANES],
                                       qpad_ref[vh], preferred_element_type=F32)

    for step in range(QK_LOOKAHEAD):
        scores(0, step)

    def body(it, carry):
        for step in range(n_steps):
            vh = step % n_vh
            scores(it, step + QK_LOOKAHEAD)
            s = s_ref[step % n_slot]
            m_old = m_ref[vh]
            m_new = jnp.maximum(m_old, jnp.max(s, axis=0, keepdims=True))
            alpha = jnp.exp2(m_old - m_new)
            p = jnp.exp2(s - m_new).astype(BF16)
            vt = vt_ref[heads[vh][3], :, pl.ds(key_rows(it, step), tk)]
            acc_ref[vh] = alpha * acc_ref[vh] + jnp.dot(vt, p, preferred_element_type=F32)
            m_ref[vh] = m_new
        return carry

    lax.fori_loop(0, n_kt // unroll, body, 0)

    def head_out(vh):
        a = acc_ref[vh]
        return a[0:HEAD_DIM, :] / a[HEAD_DIM:HEAD_DIM + 1, :]

    if diff:
        lam = lam_ref[0]
        for h in range(n_vh // 2):
            o = head_out(2 * h) - lam * head_out(2 * h + 1)
            ms = jnp.mean(o * o, axis=0, keepdims=True)
            y = (o * lax.rsqrt(ms + EPS)) * sub_ref[...]
            ot_ref[HEAD_DIM * h:HEAD_DIM * (h + 1), :] = y * post_scale
    else:
        for h in range(n_vh):
            ot_ref[HEAD_DIM * h:HEAD_DIM * (h + 1), :] = head_out(h)
    o_ref[0] = ot_ref[...].T.astype(BF16)


def _full_attention(proj, lam, sub_gain, *, q_cb, k_cb, v_cb, heads, n_kv, v_row0, unroll, diff,
                    post_scale):
    B, S, _ = proj.shape
    tq = min(TQ_FULL, S)
    kern = functools.partial(_full_attn_kernel, heads=heads, n_kv=n_kv, v_row0=v_row0,
                             unroll=unroll, diff=diff, post_scale=post_scale)
    return pl.pallas_call(
        kern,
        grid=(B, S // tq),
        in_specs=[
            pl.BlockSpec(memory_space=pltpu.SMEM),
            pl.BlockSpec((1, tq, BRANCH_W), lambda b, i: (b, i, q_cb)),
            pl.BlockSpec((1, S, BRANCH_W), lambda b, i: (b, 0, k_cb)),
            pl.BlockSpec((1, S, BRANCH_W), lambda b, i: (b, 0, v_cb)),
            pl.BlockSpec((HEAD_DIM, tq), lambda b, i: (0, 0)),
        ],
        out_specs=pl.BlockSpec((1, tq, BRANCH_W), lambda b, i: (b, i, 0)),
        out_shape=jax.ShapeDtypeStruct((B, S, BRANCH_W), BF16),
        scratch_shapes=[
            pltpu.VMEM((n_kv, HEAD_DIM + ONES_ROWS, S), BF16),
            pltpu.VMEM((len(heads), LANES, tq), BF16),
            pltpu.VMEM((len(heads), 1, tq), F32),
            pltpu.VMEM((len(heads), HEAD_DIM + ONES_ROWS, tq), F32),
            pltpu.VMEM((BRANCH_W, tq), F32),
            pltpu.VMEM((SCORE_SLOTS, min(TK_FULL, S), tq), F32),
        ],
        compiler_params=_params("arbitrary", "arbitrary"),
        name="full_attn_diff" if diff else "full_attn_gqa",
    )(lam, proj, proj, proj, sub_gain)


_HEADS_A = tuple((HEAD_DIM * h + DIFF_DIM * c, HEAD_DIM * h + DIFF_DIM * (c + 1),
                  HEAD_DIM * h + DIFF_DIM * c, h) for h in range(4) for c in range(2))
_HEADS_D = tuple((HEAD_DIM * h, HEAD_DIM * (h + 1), HEAD_DIM * (h // 2), h // 2) for h in range(4))


def _attend_blocks(blocks, want_lse):
    tq = blocks[0][0].shape[0]
    lane = lax.broadcasted_iota(jnp.int32, (tq, BRANCH_W), 1)
    in_head = [(lane >= HEAD_DIM * h) & (lane < HEAD_DIM * (h + 1)) for h in range(4)]
    scores = [[lax.dot_general(jnp.where(in_head[h], q, jnp.zeros_like(q)), kw,
                               (((1,), (1,)), ((), ())), preferred_element_type=F32)
               for h in range(4)] for q, kw, _, _ in blocks]
    stats = []
    for (_, _, _, bias_of_head), s_heads in zip(blocks, scores):
        per_head = []
        for h in range(4):
            s = bias_of_head(h, s_heads[h])
            m = jnp.max(s, axis=-1, keepdims=True)
            p = jnp.exp(s - m)
            per_head.append((m, jnp.sum(p, axis=-1, keepdims=True), p.astype(BF16)))
        stats.append(per_head)
    outs = []
    for (_, _, vw, _), per_head in zip(blocks, stats):
        o = jnp.zeros((tq, BRANCH_W), F32)
        lse = jnp.zeros((tq, BRANCH_W), F32)
        for h in range(4):
            m, l, p = per_head[h]
            of = jnp.dot(p, vw, preferred_element_type=F32)
            o = jnp.where(in_head[h], of / l, o)
            if want_lse:
                lse = jnp.where(in_head[h], m + jnp.log(l), lse)
        outs.append((o, lse))
    return outs


def _band_kernel(q_ref, k_ref, v_ref, o_ref, lse_ref):
    L = k_ref.shape[1]
    tq = q_ref.shape[1]
    sub = min(WIN_SUB, tq)
    kwin = min(sub + 2 * C_HALF, L)
    blocks = []
    for j in range(tq // sub):
        q0 = pl.program_id(2) * tq + j * sub
        ks = pl.multiple_of(jnp.clip(q0 - C_HALF, 0, L - kwin), C_HALF)
        qpos = q0 + lax.broadcasted_iota(jnp.int32, (sub, kwin), 0)
        kpos = ks + lax.broadcasted_iota(jnp.int32, (sub, kwin), 1)
        valid = jnp.abs(qpos - kpos) <= C_HALF
        blocks.append((q_ref[0, j * sub:(j + 1) * sub, :], k_ref[0, pl.ds(ks, kwin), :],
                       v_ref[0, pl.ds(ks, kwin), :],
                       lambda h, s, valid=valid: jnp.where(valid, s, NEG_INF)))
    for j, (o, lse) in enumerate(_attend_blocks(blocks, True)):
        o_ref[0, j * sub:(j + 1) * sub, :] = o.astype(BF16)
        lse_ref[0, j * sub:(j + 1) * sub, :] = lse


def _dilated_pattern(qkv, dil, cb0):
    B, _, L, _ = qkv.shape
    tq = min(TQ_WIN, L)

    def spec(rows, col):
        return pl.BlockSpec((None, 1, rows, BRANCH_W),
                            lambda b, r, i: (b, r, i if rows == tq else 0, col))

    return pl.pallas_call(
        _band_kernel,
        grid=(B, dil, L // tq),
        in_specs=[spec(tq, cb0), spec(L, cb0 + 1), spec(L, cb0 + 2)],
        out_specs=[spec(tq, 0), spec(tq, 0)],
        out_shape=[
            jax.ShapeDtypeStruct((B, dil, L, BRANCH_W), BF16),
            jax.ShapeDtypeStruct((B, dil, L, BRANCH_W), F32),
        ],
        compiler_params=_params("arbitrary", "arbitrary", "arbitrary"),
        name=f"dilated_d{dil}",
    )(qkv, qkv, qkv)


def _nbr_kernel(q_ref, k_ref, v_ref, bias_ref, o_ref):
    S = k_ref.shape[1]
    rows = S // GRID_W
    sub = B_QROWS * GRID_W
    n_sub = q_ref.shape[1] // sub
    n_steps = rows // B_QROWS
    blocks = []
    for j in range(n_sub):
        t = pl.program_id(1) * n_sub + j
        ks_row = jnp.clip(B_QROWS * t - NA_ROWS // 2, 0, rows - B_KROWS)
        ks = pl.multiple_of(ks_row * GRID_W, GRID_W)
        variant = _nbr_variant(t, n_steps)
        blocks.append((q_ref[0, j * sub:(j + 1) * sub, :],
                       k_ref[0, pl.ds(ks, B_KROWS * GRID_W), :],
                       v_ref[0, pl.ds(ks, B_KROWS * GRID_W), :],
                       lambda h, s, variant=variant: s + bias_ref[variant, h]))
    for j, (o, _) in enumerate(_attend_blocks(blocks, False)):
        o_ref[0, j * sub:(j + 1) * sub, :] = o.astype(BF16)


def _nbr_variant(t, n_steps):
    return jnp.where(t < 2, t, jnp.where(t >= n_steps - 2, t - (n_steps - 5), 2))


def _nbr_bias(rpb, S):
    rows = S // GRID_W
    n_steps = rows // B_QROWS
    steps = np.array([0, 1, 2, n_steps - 2, n_steps - 1])
    r = (steps[:, None] * B_QROWS + np.arange(B_QROWS)[None, :])
    ks_row = np.clip(steps * B_QROWS - NA_ROWS // 2, 0, rows - B_KROWS)
    kabs = ks_row[:, None] + np.arange(B_KROWS)[None, :]
    rs = np.clip(r - NA_ROWS // 2, 0, rows - NA_ROWS)
    row_ok = (kabs[:, None, :] >= rs[:, :, None]) & (kabs[:, None, :] < rs[:, :, None] + NA_ROWS)
    dr = np.clip(kabs[:, None, :] - r[:, :, None] + NA_ROWS - 1, 0, 2 * NA_ROWS - 2)
    c = np.arange(GRID_W)
    col_start = np.clip(c - NA_COLS // 2, 0, GRID_W - NA_COLS)
    col_ok = (c[None, :] >= col_start[:, None]) & (c[None, :] < col_start[:, None] + NA_COLS)
    dc = np.clip(c[None, :] - c[:, None] + NA_COLS - 1, 0, 2 * NA_COLS - 2)
    H = rpb.shape[0]
    onehot = (dc.reshape(-1)[None, :] == np.arange(2 * NA_COLS - 1)[:, None]).astype(np.float32)
    by_col = jnp.einsum("hdc,cn->hdn", rpb.astype(F32), jnp.asarray(onehot),
                        precision=lax.Precision.HIGHEST).reshape(H, 2 * NA_ROWS - 1, GRID_W, GRID_W)
    vals = jnp.stack([by_col[:, int(d)] for d in dr.reshape(-1)], axis=1)
    vals = vals.reshape(H, 5, B_QROWS, B_KROWS, GRID_W, GRID_W)
    vals = vals.transpose(0, 1, 2, 4, 3, 5)
    OK = row_ok[:, :, None, :, None] & col_ok[None, None, :, None, :]
    vals = jnp.where(jnp.asarray(OK)[None], vals, NEG_INF)
    return vals.reshape(H, 5, B_QROWS * GRID_W, B_KROWS * GRID_W).transpose(1, 0, 2, 3)


def _neighbourhood(proj, bias):
    B, S, _ = proj.shape
    sub = B_QROWS * GRID_W
    tq = min(NBR_BLOCKS_PER_STEP * sub, S)
    return pl.pallas_call(
        _nbr_kernel,
        grid=(B, S // tq),
        in_specs=[
            pl.BlockSpec((1, tq, BRANCH_W), lambda b, t: (b, t, CB_BQ)),
            pl.BlockSpec((1, S, BRANCH_W), lambda b, t: (b, 0, CB_BK)),
            pl.BlockSpec((1, S, BRANCH_W), lambda b, t: (b, 0, CB_BV)),
            pl.BlockSpec((5, 4, sub, B_KROWS * GRID_W), lambda b, t: (0, 0, 0, 0)),
        ],
        out_specs=pl.BlockSpec((1, tq, BRANCH_W), lambda b, t: (b, t, 0)),
        out_shape=jax.ShapeDtypeStruct((B, S, BRANCH_W), BF16),
        compiler_params=_params("arbitrary", "arbitrary"),
        name="neighbourhood",
    )(proj, proj, proj, bias)


def _merge_kernel(x_ref, gn_ref, oa_ref, ob_ref, od_ref, oc1_ref, oc2_ref, oc3_ref, l1_ref, l2_ref,
                  l3_ref, wg_ref, wb_ref, wo_ref, y_ref, il_ref):
    tm = x_ref.shape[1]
    x = x_ref[0]
    ms = jnp.mean(x * x, axis=-1, keepdims=True)
    h = ((x * lax.rsqrt(ms + EPS)) * gn_ref[...]).astype(BF16)

    def natural(ref):
        dil = ref.shape[0]
        if dil == 1:
            return ref[0].astype(F32)
        for r in range(dil):
            v = ref[r].astype(F32)
            for half in range(2):
                il_ref[half, pl.ds(r, tm // dil, stride=dil), :] = v[:, half * LANES:(half + 1) * LANES]
        return jnp.concatenate([il_ref[0], il_ref[1]], axis=1)

    l1, l2, l3 = natural(l1_ref), natural(l2_ref), natural(l3_ref)
    m = jnp.maximum(jnp.maximum(l1, l2), l3)
    e1, e2, e3 = jnp.exp(l1 - m), jnp.exp(l2 - m), jnp.exp(l3 - m)
    oc = (e1 * natural(oc1_ref) + e2 * natural(oc2_ref) + e3 * natural(oc3_ref)) / (e1 + e2 + e3)

    oc = oc.astype(BF16)
    rc = min(MERGE_ROW_CHUNK, tm)

    def gated_sum(c):
        rows = slice(c * rc, (c + 1) * rc)
        merged = None
        for i, o in enumerate((oa_ref[0, rows, :], ob_ref[0, rows, :], oc[rows, :], od_ref[0, rows, :])):
            gate = 0.5 + 0.5 * jnp.tanh(0.5 * jnp.dot(h[rows, :], wg_ref[i], preferred_element_type=F32))
            term = gate * jnp.dot(o, wb_ref[i], preferred_element_type=F32)
            merged = term if merged is None else merged + term
        return merged.astype(BF16)

    n_c = tm // rc
    pending = gated_sum(0)
    for c in range(n_c):
        ready = pending
        if c + 1 < n_c:
            pending = gated_sum(c + 1)
        rows = slice(c * rc, (c + 1) * rc)
        y_ref[0, rows, :] = x[rows, :] + jnp.dot(ready, wo_ref[...], preferred_element_type=F32)


def _merge(x, gain, o_a, o_b, o_d, oc, lse, w_gate, w_branch, w_out):
    B, S, _ = x.shape
    tm = min(TM_MERGE, S)
    tok256 = pl.BlockSpec((1, tm, BRANCH_W), lambda b, i: (b, i, 0))

    def grouped(arr):
        dil = arr.shape[1]
        return pl.BlockSpec((None, dil, tm // dil, BRANCH_W), lambda b, i: (b, 0, i, 0))

    def resident(shape):
        return pl.BlockSpec(shape, lambda b, i: (0,) * len(shape), pipeline_mode=pl.Buffered(1))

    return pl.pallas_call(
        _merge_kernel,
        grid=(B, S // tm),
        in_specs=[pl.BlockSpec((1, tm, D_MODEL), lambda b, i: (b, i, 0)),
                  resident((1, D_MODEL)),
                  tok256, tok256, tok256,
                  grouped(oc[0]), grouped(oc[1]), grouped(oc[2]),
                  grouped(lse[0]), grouped(lse[1]), grouped(lse[2]),
                  resident((N_BRANCH, D_MODEL, D_MODEL)),
                  resident((N_BRANCH, BRANCH_W, D_MODEL)),
                  resident((D_MODEL, D_MODEL))],
        out_specs=pl.BlockSpec((1, tm, D_MODEL), lambda b, i: (b, i, 0)),
        out_shape=jax.ShapeDtypeStruct((B, S, D_MODEL), F32),
        scratch_shapes=[pltpu.VMEM((BRANCH_W // LANES, tm, LANES), F32)],
        compiler_params=_params("arbitrary", "arbitrary"),
        name="merge_out",
    )(x, gain, o_a, o_b, o_d, oc[0], oc[1], oc[2], lse[0], lse[1], lse[2],
      w_gate, w_branch, w_out)


def _mlp_kernel(x_ref, xp_ref, xn_ref, g_ref, wu_ref, cw_ref, cb_ref, wd_ref, gf_ref, y_ref,
                hext_ref, u_ref, act_ref, *, final_norm):
    i = pl.program_id(1)
    tm = x_ref.shape[1]
    halo = MLP_HALO
    rc = min(MLP_ROW_CHUNK, tm)
    ext = rc + 2 * halo
    c = FF_CHUNK
    n_f = D_FF // c

    def norm(x, gain):
        ms = jnp.mean(x * x, axis=-1, keepdims=True)
        return (x * lax.rsqrt(ms + EPS)) * gain

    hext_ref[0:halo, :] = jnp.where(i > 0, norm(xp_ref[0], g_ref[...]), 0.0).astype(BF16)
    hext_ref[halo:halo + tm, :] = norm(x_ref[0], g_ref[...]).astype(BF16)
    hext_ref[halo + tm:, :] = jnp.where(i < pl.num_programs(1) - 1,
                                        norm(xn_ref[0], g_ref[...]), 0.0).astype(BF16)

    def up(r0, f):
        hx = hext_ref[pl.ds(r0, ext), :]
        for half in range(2):
            lo = half * D_FF + f * c
            u_ref[f % 2, half] = jnp.dot(hx, wu_ref[:, lo:lo + c], preferred_element_type=F32)

    def conv(u, lo):
        before = pltpu.roll(u, 1, 0)[halo:halo + rc, :]
        after = pltpu.roll(u, ext - 1, 0)[halo:halo + rc, :]
        return (before * cw_ref[0:1, lo:lo + c] + u[halo:halo + rc, :] * cw_ref[1:2, lo:lo + c]
                + after * cw_ref[2:3, lo:lo + c] + cb_ref[:, lo:lo + c])

    def row_chunk(ci, carry):
        r0 = pl.multiple_of(ci * rc, rc)
        up(r0, 0)
        for f in range(n_f):
            if f + 1 < n_f:
                up(r0, f + 1)
            val = conv(u_ref[f % 2, 0], f * c)
            gt = conv(u_ref[f % 2, 1], D_FF + f * c)
            act = (0.5 * gt * (1.0 + lax.erf(gt * math.sqrt(0.5)))) * val
            act_ref[:, f * c:(f + 1) * c] = act.astype(BF16)
        y = x_ref[0, pl.ds(r0, rc), :] + jnp.dot(act_ref[...], wd_ref[...],
                                                 preferred_element_type=F32)
        if final_norm:
            y = norm(y, gf_ref[...])
        y_ref[0, pl.ds(r0, rc), :] = y
        return carry

    lax.fori_loop(0, tm // rc, row_chunk, 0)


def _mlp(x, gain, w_up, conv_w, conv_b, w_down, gain_final, final_norm):
    B, S, _ = x.shape
    tm = min(TM_MLP, S)
    rc = min(MLP_ROW_CHUNK, tm)
    hb = tm // MLP_HALO
    n_i = S // tm
    kern = functools.partial(_mlp_kernel, final_norm=final_norm)

    def resident(shape):
        return pl.BlockSpec(shape, lambda b, i: (0,) * len(shape), pipeline_mode=pl.Buffered(1))

    return pl.pallas_call(
        kern,
        grid=(B, n_i),
        in_specs=[
            pl.BlockSpec((1, tm, D_MODEL), lambda b, i: (b, i, 0)),
            pl.BlockSpec((1, MLP_HALO, D_MODEL), lambda b, i: (b, jnp.maximum(i * hb - 1, 0), 0)),
            pl.BlockSpec((1, MLP_HALO, D_MODEL),
                         lambda b, i: (b, jnp.minimum((i + 1) * hb, S // MLP_HALO - 1), 0)),
            resident((1, D_MODEL)),
            resident((D_MODEL, 2 * D_FF)),
            resident((3, 2 * D_FF)),
            resident((1, 2 * D_FF)),
            resident((D_FF, D_MODEL)),
            resident((1, D_MODEL)),
        ],
        out_specs=pl.BlockSpec((1, tm, D_MODEL), lambda b, i: (b, i, 0)),
        out_shape=jax.ShapeDtypeStruct((B, S, D_MODEL), F32),
        scratch_shapes=[
            pltpu.VMEM((tm + 2 * MLP_HALO, D_MODEL), BF16),
            pltpu.VMEM((2, 2, rc + 2 * MLP_HALO, FF_CHUNK), F32),
            pltpu.VMEM((rc, D_FF), BF16),
        ],
        compiler_params=_params("arbitrary", "arbitrary"),
        name="mlp",
    )(x, x, x, gain, w_up, conv_w, conv_b, w_down, gain_final)


def _layer(x, l, tables, gmat, p):
    lam_init = 0.8 - 0.6 * math.exp(-0.3 * l)
    S = x.shape[1]
    w_in = p["w_in"][l][:, :N_MIX_COLS].astype(BF16)
    w_gate = p["w_in"][l][:, N_MIX_COLS:].reshape(D_MODEL, N_BRANCH, D_MODEL)
    w_gate = w_gate.transpose(1, 0, 2).astype(BF16)
    qk_gain = jnp.stack([jnp.tile(p["qk_norm"][l, 0], 4), jnp.tile(p["qk_norm"][l, 1], 4)]).astype(F32)
    proj, c_by4, c_by16 = _in_proj(x, p["norm_attn"][l][None, :], w_in, tables, gmat, qk_gain)

    lv = p["diff_lambda"][l].astype(F32)
    lam = (jnp.exp(jnp.sum(lv[0] * lv[1])) - jnp.exp(jnp.sum(lv[2] * lv[3])) + lam_init).reshape(1)
    tq = min(TQ_FULL, S)
    sub_gain = jnp.broadcast_to(p["diff_subln"][l].astype(F32)[:, None], (HEAD_DIM, tq))
    o_a = _full_attention(proj, lam, sub_gain, q_cb=CB_AQ, k_cb=CB_AK, v_cb=CB_AV, heads=_HEADS_A,
                          n_kv=4, v_row0=0, unroll=4, diff=True, post_scale=1.0 - lam_init)
    o_d = _full_attention(proj, lam, sub_gain, q_cb=CB_DQ, k_cb=CB_DKV, v_cb=CB_DKV, heads=_HEADS_D,
                          n_kv=2, v_row0=2 * HEAD_DIM, unroll=8, diff=False, post_scale=1.0)
    o_b = _neighbourhood(proj, _nbr_bias(p["na_rpb"][l], S))
    oc, lse = zip(_dilated_pattern(proj[:, None], 1, CB_CQ), _dilated_pattern(c_by4, 4, 0),
                  _dilated_pattern(c_by16, 16, 0))
    x = _merge(x, p["norm_attn"][l][None, :], o_a, o_b, o_d, oc, lse, w_gate,
               p["w_branch"][l].astype(BF16), p["w_out"][l].astype(BF16))
    return _mlp(x, p["norm_mlp"][l][None, :], p["w_up"][l].astype(BF16), p["conv_w"][l],
                p["conv_b"][l][None, :], p["w_down"][l].astype(BF16), p["norm_final"][None, :],
                final_norm=(l == DEPTH - 1))


def _trunk(x, tables, gmat, p):
    for l in range(DEPTH):
        x = _layer(x, l, tables, gmat, p)
    return x


def _group_mean_matrix():
    head = np.arange(BRANCH_W) // HEAD_DIM
    return jnp.asarray((head[:, None] == head[None, :]) / HEAD_DIM, dtype=BF16)


def kernel(x_prompt, x_sample, norm_attn, w_in, diff_lambda, diff_subln, na_rpb, qk_norm, w_branch,
           w_out, norm_mlp, w_up, conv_w, conv_b, w_down, norm_final):
    p = dict(norm_attn=norm_attn, w_in=w_in, diff_lambda=diff_lambda, diff_subln=diff_subln,
             na_rpb=na_rpb, qk_norm=qk_norm, w_branch=w_branch, w_out=w_out, norm_mlp=norm_mlp,
             w_up=w_up, conv_w=conv_w, conv_b=conv_b, w_down=w_down, norm_final=norm_final)
    gmat = _group_mean_matrix()
    outs = []
    for x in (x_prompt, x_sample):
        tables = _rope_tables(x.shape[1])
        outs.append(_trunk(x, tables, gmat, p))
    return tuple(outs)
```

```python
import functools
import math

import jax
import jax.numpy as jnp
import numpy as np
from jax import lax
from jax.experimental import pallas as pl
from jax.experimental.pallas import tpu as pltpu

F32 = jnp.float32
BF16 = jnp.bfloat16

D_MODEL = 1024
HEAD_DIM = 64
LANES = 128
DIFF_DIM = 32
BRANCH_W = 256
N_BRANCH = 4
GRID_W = 64
NA_ROWS = 8
NA_COLS = 16
C_PATTERNS = ((128, 1), (512, 4), (2048, 16))
ROPE_THETA = 500000.0
AXIAL_THETA = 10000.0
D_FF = 2816
EPS = 1e-6
NEG_INF = -1e30
DEPTH = 2
N_MIX_COLS = 2816
N_COL_BLOCKS = N_MIX_COLS // BRANCH_W
LOG2E = 1.4426950408889634

(CB_AQ, CB_AK, CB_AV, CB_BQ, CB_BK, CB_BV, CB_CQ, CB_CK, CB_CV, CB_DQ, CB_DKV) = range(N_COL_BLOCKS)

VMEM_LIMIT = 56 * 1024 * 1024

TM_PROJ = 1024
PROJ_ROW_CHUNK = 512
TQ_FULL = 512
TK_FULL = 512
ONES_ROWS = 16
QK_LOOKAHEAD = 2
SCORE_SLOTS = 4
TQ_WIN = 512
WIN_SUB = 128
NBR_BLOCKS_PER_STEP = 4
C_HALF = 64
B_QROWS = 2
B_KROWS = 10
TM_MERGE = 1024
MERGE_ROW_CHUNK = 256
TM_MLP = 1024
MLP_ROW_CHUNK = 1024
MLP_HALO = 16
FF_CHUNK = 256


def _params(*sem):
    return pltpu.CompilerParams(dimension_semantics=sem, vmem_limit_bytes=VMEM_LIMIT)


def _rotate_pairs(y, cos, sin, half, group):
    lane = lax.broadcasted_iota(jnp.int32, y.shape, 1) % group
    second = (lane >= half) & (lane < 2 * half)
    from_below = pltpu.roll(y, half, 1)
    from_above = pltpu.roll(y, y.shape[1] - half, 1)
    return y * cos + jnp.where(second, from_below, from_above) * sin


def _head_rms(y, gmat, gain):
    ms = jnp.dot((y * y).astype(BF16), gmat, preferred_element_type=F32)
    return y * lax.rsqrt(ms + EPS) * gain


def _in_proj_kernel(x_ref, g_ref, w_ref, tab_ref, gmat_ref, qkg_ref, o_ref, c4_ref, c16_ref,
                    h_ref, stage_ref):
    j = pl.program_id(2)

    @pl.when(j == 0)
    def _():
        x = x_ref[0]
        ms = jnp.mean(x * x, axis=-1, keepdims=True)
        h_ref[...] = ((x * lax.rsqrt(ms + EPS)) * g_ref[...]).astype(BF16)

    scale_a = DIFF_DIM ** -0.5 * LOG2E
    scale_d = HEAD_DIM ** -0.5 * LOG2E
    scale_w = HEAD_DIM ** -0.5
    tm = h_ref.shape[0]
    rc = min(PROJ_ROW_CHUNK, tm)

    def project(epilogue, regroup=False):
        for c in range(tm // rc):
            rows = slice(c * rc, (c + 1) * rc)
            acc = jnp.dot(h_ref[rows, :], w_ref[...], preferred_element_type=F32)
            y = epilogue(acc, rows)
            o_ref[0, rows, :] = y.astype(BF16)
            if regroup:
                for half in range(2):
                    stage_ref[half, rows, :] = y[:, half * LANES:(half + 1) * LANES]
        if regroup:
            for dil, ref in ((4, c4_ref), (16, c16_ref)):
                for r in range(dil):
                    for half in range(2):
                        part = stage_ref[half, pl.ds(r, tm // dil, stride=dil), :]
                        ref[0, r, :, half * LANES:(half + 1) * LANES] = part.astype(BF16)

    def rope_a(y, rows):
        return _rotate_pairs(y, tab_ref[0, rows, :], tab_ref[1, rows, :], 4, 32)

    def rope_c(y, rows):
        return _rotate_pairs(y, tab_ref[2, rows, :], tab_ref[3, rows, :], 8, 64)

    def rope_d(y, rows):
        return _rotate_pairs(y, tab_ref[4, rows, :], tab_ref[5, rows, :], 16, 32)

    @pl.when(j == CB_AQ)
    def _():
        project(lambda acc, rows: rope_a(acc, rows) * scale_a)

    @pl.when(j == CB_AK)
    def _():
        project(rope_a)

    @pl.when((j == CB_AV) | (j == CB_BK) | (j == CB_BV))
    def _():
        project(lambda acc, rows: acc)

    @pl.when(j == CB_BQ)
    def _():
        project(lambda acc, rows: acc * scale_w)

    @pl.when(j == CB_CQ)
    def _():
        project(lambda acc, rows: rope_c(acc, rows) * scale_w, regroup=True)

    @pl.when(j == CB_CK)
    def _():
        project(rope_c, regroup=True)

    @pl.when(j == CB_CV)
    def _():
        project(lambda acc, rows: acc, regroup=True)

    @pl.when(j == CB_DQ)
    def _():
        project(lambda acc, rows:
                rope_d(_head_rms(acc, gmat_ref[...], qkg_ref[0:1, :]), rows) * scale_d)

    @pl.when(j == CB_DKV)
    def _():
        def keys_and_values(acc, rows):
            lane = lax.broadcasted_iota(jnp.int32, acc.shape, 1)
            keys = rope_d(_head_rms(acc, gmat_ref[...], qkg_ref[1:2, :]), rows)
            return jnp.where(lane < 2 * HEAD_DIM, keys, acc)

        project(keys_and_values)


def _in_proj(x, gain, w_bf16, tables, gmat, qk_gain):
    B, S, _ = x.shape
    tm = min(TM_PROJ, S)
    grid = (S // tm, B, N_COL_BLOCKS)

    def regrouped(dil):
        return pl.BlockSpec((1, dil, tm // dil, BRANCH_W),
                            lambda i, b, j: (b, 0, i, jnp.clip(j - CB_CQ, 0, 2)))

    return pl.pallas_call(
        _in_proj_kernel,
        grid=grid,
        in_specs=[
            pl.BlockSpec((1, tm, D_MODEL), lambda i, b, j: (b, i, 0)),
            pl.BlockSpec((1, D_MODEL), lambda i, b, j: (0, 0)),
            pl.BlockSpec((D_MODEL, BRANCH_W), lambda i, b, j: (0, j)),
            pl.BlockSpec((6, tm, BRANCH_W), lambda i, b, j: (0, i, 0)),
            pl.BlockSpec((BRANCH_W, BRANCH_W), lambda i, b, j: (0, 0)),
            pl.BlockSpec((2, BRANCH_W), lambda i, b, j: (0, 0)),
        ],
        out_specs=[pl.BlockSpec((1, tm, BRANCH_W), lambda i, b, j: (b, i, j)),
                   regrouped(4), regrouped(16)],
        out_shape=[jax.ShapeDtypeStruct((B, S, N_MIX_COLS), BF16),
                   jax.ShapeDtypeStruct((B, 4, S // 4, 3 * BRANCH_W), BF16),
                   jax.ShapeDtypeStruct((B, 16, S // 16, 3 * BRANCH_W), BF16)],
        scratch_shapes=[pltpu.VMEM((tm, D_MODEL), BF16),
                        pltpu.VMEM((BRANCH_W // LANES, tm, LANES), F32)],
        compiler_params=_params("arbitrary", "arbitrary", "arbitrary"),
        name="in_proj",
    )(x, gain, w_bf16, tables, gmat, qk_gain)


def _rope_tables(S):
    pos = jnp.arange(S, dtype=jnp.int32)
    lane = np.arange(BRANCH_W)

    def build(group, half, ang_of_lane):
        m = lane % group
        first = m < half
        second = (m >= half) & (m < 2 * half)
        ang = ang_of_lane
        cos = jnp.where(jnp.asarray(first | second)[None, :], jnp.cos(ang), 1.0)
        sin = jnp.where(jnp.asarray(second)[None, :], jnp.sin(ang),
                        jnp.where(jnp.asarray(first)[None, :], -jnp.sin(ang), 0.0))
        return cos.astype(F32), sin.astype(F32)

    def angles(p, theta, half, idx):
        inv = jnp.exp(-math.log(theta) * jnp.arange(half, dtype=F32) / half)
        ang = p.astype(F32)[:, None] * inv[None, :]
        return ang[:, idx]

    ca, sa = build(32, 4, angles(pos, ROPE_THETA, 4, (lane % 32) % 4))
    cc, sc = build(64, 8, angles(pos, ROPE_THETA, 8, (lane % 64) % 8))
    idx = (lane % 32) % 16
    ang_row = angles(pos // GRID_W, AXIAL_THETA, 16, idx)
    ang_col = angles(pos % GRID_W, AXIAL_THETA, 16, idx)
    ang_d = jnp.where(jnp.asarray((lane % 64) < 32)[None, :], ang_row, ang_col)
    cd, sd = build(32, 16, ang_d)
    return jnp.stack([ca, sa, cc, sc, cd, sd], axis=0)


def _full_attn_kernel(lam_ref, q_ref, k_ref, v_ref, sub_ref, o_ref, vt_ref, qpad_ref, m_ref, acc_ref,
                      ot_ref, s_ref, *, heads, n_kv, v_row0, unroll, diff, post_scale):
    S = k_ref.shape[1]
    tq = q_ref.shape[1]
    tk = min(TK_FULL, S)
    n_kt = S // tk
    n_vh = len(heads)
    vrows = vt_ref.shape[1]

    @pl.when(pl.program_id(1) == 0)
    def _():
        chunk = min(512, S)

        def tr(c, carry):
            r0 = pl.multiple_of(c * chunk, chunk)
            vct = v_ref[0, pl.ds(r0, chunk), :].astype(F32).T
            for g in range(n_kv):
                lo = v_row0 + HEAD_DIM * g
                vt_ref[g, 0:HEAD_DIM, pl.ds(r0, chunk)] = vct[lo:lo + HEAD_DIM, :].astype(BF16)
                vt_ref[g, HEAD_DIM:vrows, pl.ds(r0, chunk)] = jnp.ones((vrows - HEAD_DIM, chunk), BF16)
            return carry

        lax.fori_loop(0, S // chunk, tr, 0)

    qt = q_ref[0].astype(F32).T.astype(BF16)
    for vh, (slo, shi, dlo, _) in enumerate(heads):
        qpad_ref[vh] = jnp.zeros((LANES, tq), BF16)
        qpad_ref[vh, dlo % LANES:dlo % LANES + (shi - slo), :] = qt[slo:shi, :]
    m_ref[...] = jnp.full(m_ref.shape, NEG_INF, F32)
    acc_ref[...] = jnp.zeros(acc_ref.shape, F32)

    n_steps = unroll * n_vh
    n_slot = s_ref.shape[0]
    assert n_steps % n_slot == 0 and QK_LOOKAHEAD < n_slot and n_kt % unroll == 0

    def key_rows(it, step):
        kt = jnp.minimum(it * unroll + step // n_vh, n_kt - 1)
        return pl.multiple_of(kt * tk, tk)

    def scores(it, step):
        vh = step % n_vh
        lane0 = heads[vh][2] // LANES * LANES
        s_ref[step % n_slot] = jnp.dot(k_ref[0, pl.ds(key_rows(it, step), tk), lane0:lane0 + LANES],
                                       qpad_ref[vh], preferred_element_type=F32)

    for step in range(QK_LOOKAHEAD):
        scores(0, step)

    def body(it, carry):
        for step in range(n_steps):
            vh = step % n_vh
            scores(it, step + QK_LOOKAHEAD)
            s = s_ref[step % n_slot]
            m_old = m_ref[vh]
            m_new = jnp.maximum(m_old, jnp.max(s, axis=0, keepdims=True))
            alpha = jnp.exp2(m_old - m_new)
            p = jnp.exp2(s - m_new).astype(BF16)
            vt = vt_ref[heads[vh][3], :, pl.ds(key_rows(it, step), tk)]
            acc_ref[vh] = alpha * acc_ref[vh] + jnp.dot(vt, p, preferred_element_type=F32)
            m_ref[vh] = m_new
        return carry

    lax.fori_loop(0, n_kt // unroll, body, 0)

    def head_out(vh):
        a = acc_ref[vh]
        return a[0:HEAD_DIM, :] / a[HEAD_DIM:HEAD_DIM + 1, :]

    if diff:
        lam = lam_ref[0]
        for h in range(n_vh // 2):
            o = head_out(2 * h) - lam * head_out(2 * h + 1)
            ms = jnp.mean(o * o, axis=0, keepdims=True)
            y = (o * lax.rsqrt(ms + EPS)) * sub_ref[...]
            ot_ref[HEAD_DIM * h:HEAD_DIM * (h + 1), :] = y * post_scale
    else:
        for h in range(n_vh):
            ot_ref[HEAD_DIM * h:HEAD_DIM * (h + 1), :] = head_out(h)
    o_ref[0] = ot_ref[...].T.astype(BF16)


def _full_attention(proj, lam, sub_gain, *, q_cb, k_cb, v_cb, heads, n_kv, v_row0, unroll, diff,
                    post_scale):
    B, S, _ = proj.shape
    tq = min(TQ_FULL, S)
    kern = functools.partial(_full_attn_kernel, heads=heads, n_kv=n_kv, v_row0=v_row0,
                             unroll=unroll, diff=diff, post_scale=post_scale)
    return pl.pallas_call(
        kern,
        grid=(B, S // tq),
        in_specs=[
            pl.BlockSpec(memory_space=pltpu.SMEM),
            pl.BlockSpec((1, tq, BRANCH_W), lambda b, i: (b, i, q_cb)),
            pl.BlockSpec((1, S, BRANCH_W), lambda b, i: (b, 0, k_cb)),
            pl.BlockSpec((1, S, BRANCH_W), lambda b, i: (b, 0, v_cb)),
            pl.BlockSpec((HEAD_DIM, tq), lambda b, i: (0, 0)),
        ],
        out_specs=pl.BlockSpec((1, tq, BRANCH_W), lambda b, i: (b, i, 0)),
        out_shape=jax.ShapeDtypeStruct((B, S, BRANCH_W), BF16),
        scratch_shapes=[
            pltpu.VMEM((n_kv, HEAD_DIM + ONES_ROWS, S), BF16),
            pltpu.VMEM((len(heads), LANES, tq), BF16),
            pltpu.VMEM((len(heads), 1, tq), F32),
            pltpu.VMEM((len(heads), HEAD_DIM + ONES_ROWS, tq), F32),
            pltpu.VMEM((BRANCH_W, tq), F32),
            pltpu.VMEM((SCORE_SLOTS, min(TK_FULL, S), tq), F32),
        ],
        compiler_params=_params("arbitrary", "arbitrary"),
        name="full_attn_diff" if diff else "full_attn_gqa",
    )(lam, proj, proj, proj, sub_gain)


_HEADS_A = tuple((HEAD_DIM * h + DIFF_DIM * c, HEAD_DIM * h + DIFF_DIM * (c + 1),
                  HEAD_DIM * h + DIFF_DIM * c, h) for h in range(4) for c in range(2))
_HEADS_D = tuple((HEAD_DIM * h, HEAD_DIM * (h + 1), HEAD_DIM * (h // 2), h // 2) for h in range(4))


def _attend_blocks(blocks, want_lse):
    tq = blocks[0][0].shape[0]
    lane = lax.broadcasted_iota(jnp.int32, (tq, BRANCH_W), 1)
    in_head = [(lane >= HEAD_DIM * h) & (lane < HEAD_DIM * (h + 1)) for h in range(4)]
    def pair(x, h):
        return x[:, h // 2 * LANES:(h // 2 + 1) * LANES]

    scores = [[lax.dot_general(pair(jnp.where(in_head[h], q, jnp.zeros_like(q)), h), pair(kw, h),
                               (((1,), (1,)), ((), ())), preferred_element_type=F32)
               for h in range(4)] for q, kw, _, _ in blocks]
    stats = []
    for (_, _, _, bias_of_head), s_heads in zip(blocks, scores):
        per_head = []
        for h in range(4):
            s = bias_of_head(h, s_heads[h])
            m = jnp.max(s, axis=-1, keepdims=True)
            p = jnp.exp(s - m)
            per_head.append((m, jnp.sum(p, axis=-1, keepdims=True), p.astype(BF16)))
        stats.append(per_head)
    outs = []
    for (_, _, vw, _), per_head in zip(blocks, stats):
        o = jnp.zeros((tq, BRANCH_W), F32)
        lse = jnp.zeros((tq, BRANCH_W), F32)
        for h in range(4):
            m, l, p = per_head[h]
            of = jnp.dot(p, vw, preferred_element_type=F32)
            o = jnp.where(in_head[h], of / l, o)
            if want_lse:
                lse = jnp.where(in_head[h], m + jnp.log(l), lse)
        outs.append((o, lse))
    return outs


def _band_kernel(q_ref, k_ref, v_ref, o_ref, lse_ref):
    L = k_ref.shape[1]
    tq = q_ref.shape[1]
    sub = min(WIN_SUB, tq)
    kwin = min(sub + 2 * C_HALF, L)
    blocks = []
    for j in range(tq // sub):
        q0 = pl.program_id(2) * tq + j * sub
        ks = pl.multiple_of(jnp.clip(q0 - C_HALF, 0, L - kwin), C_HALF)
        qpos = q0 + lax.broadcasted_iota(jnp.int32, (sub, kwin), 0)
        kpos = ks + lax.broadcasted_iota(jnp.int32, (sub, kwin), 1)
        valid = jnp.abs(qpos - kpos) <= C_HALF
        blocks.append((q_ref[0, j * sub:(j + 1) * sub, :], k_ref[0, pl.ds(ks, kwin), :],
                       v_ref[0, pl.ds(ks, kwin), :],
                       lambda h, s, valid=valid: jnp.where(valid, s, NEG_INF)))
    for j, (o, lse) in enumerate(_attend_blocks(blocks, True)):
        o_ref[0, j * sub:(j + 1) * sub, :] = o.astype(BF16)
        lse_ref[0, j * sub:(j + 1) * sub, :] = lse


def _dilated_pattern(qkv, dil, cb0):
    B, _, L, _ = qkv.shape
    tq = min(TQ_WIN, L)

    def spec(rows, col):
        return pl.BlockSpec((None, 1, rows, BRANCH_W),
                            lambda b, r, i: (b, r, i if rows == tq else 0, col))

    return pl.pallas_call(
        _band_kernel,
        grid=(B, dil, L // tq),
        in_specs=[spec(tq, cb0), spec(L, cb0 + 1), spec(L, cb0 + 2)],
        out_specs=[spec(tq, 0), spec(tq, 0)],
        out_shape=[
            jax.ShapeDtypeStruct((B, dil, L, BRANCH_W), BF16),
            jax.ShapeDtypeStruct((B, dil, L, BRANCH_W), F32),
        ],
        compiler_params=_params("arbitrary", "arbitrary", "arbitrary"),
        name=f"dilated_d{dil}",
    )(qkv, qkv, qkv)


def _nbr_kernel(q_ref, k_ref, v_ref, bias_ref, o_ref):
    S = k_ref.shape[1]
    rows = S // GRID_W
    sub = B_QROWS * GRID_W
    n_sub = q_ref.shape[1] // sub
    n_steps = rows // B_QROWS
    blocks = []
    for j in range(n_sub):
        t = pl.program_id(1) * n_sub + j
        ks_row = jnp.clip(B_QROWS * t - NA_ROWS // 2, 0, rows - B_KROWS)
        ks = pl.multiple_of(ks_row * GRID_W, GRID_W)
        variant = _nbr_variant(t, n_steps)
        blocks.append((q_ref[0, j * sub:(j + 1) * sub, :],
                       k_ref[0, pl.ds(ks, B_KROWS * GRID_W), :],
                       v_ref[0, pl.ds(ks, B_KROWS * GRID_W), :],
                       lambda h, s, variant=variant: s + bias_ref[variant, h]))
    for j, (o, _) in enumerate(_attend_blocks(blocks, False)):
        o_ref[0, j * sub:(j + 1) * sub, :] = o.astype(BF16)


def _nbr_variant(t, n_steps):
    return jnp.where(t < 2, t, jnp.where(t >= n_steps - 2, t - (n_steps - 5), 2))


def _nbr_bias(rpb, S):
    rows = S // GRID_W
    n_steps = rows // B_QROWS
    steps = np.array([0, 1, 2, n_steps - 2, n_steps - 1])
    r = (steps[:, None] * B_QROWS + np.arange(B_QROWS)[None, :])
    ks_row = np.clip(steps * B_QROWS - NA_ROWS // 2, 0, rows - B_KROWS)
    kabs = ks_row[:, None] + np.arange(B_KROWS)[None, :]
    rs = np.clip(r - NA_ROWS // 2, 0, rows - NA_ROWS)
    row_ok = (kabs[:, None, :] >= rs[:, :, None]) & (kabs[:, None, :] < rs[:, :, None] + NA_ROWS)
    dr = np.clip(kabs[:, None, :] - r[:, :, None] + NA_ROWS - 1, 0, 2 * NA_ROWS - 2)
    c = np.arange(GRID_W)
    col_start = np.clip(c - NA_COLS // 2, 0, GRID_W - NA_COLS)
    col_ok = (c[None, :] >= col_start[:, None]) & (c[None, :] < col_start[:, None] + NA_COLS)
    dc = np.clip(c[None, :] - c[:, None] + NA_COLS - 1, 0, 2 * NA_COLS - 2)
    H = rpb.shape[0]
    onehot = (dc.reshape(-1)[None, :] == np.arange(2 * NA_COLS - 1)[:, None]).astype(np.float32)
    by_col = jnp.einsum("hdc,cn->hdn", rpb.astype(F32), jnp.asarray(onehot),
                        precision=lax.Precision.HIGHEST).reshape(H, 2 * NA_ROWS - 1, GRID_W, GRID_W)
    vals = jnp.stack([by_col[:, int(d)] for d in dr.reshape(-1)], axis=1)
    vals = vals.reshape(H, 5, B_QROWS, B_KROWS, GRID_W, GRID_W)
    vals = vals.transpose(0, 1, 2, 4, 3, 5)
    OK = row_ok[:, :, None, :, None] & col_ok[None, None, :, None, :]
    vals = jnp.where(jnp.asarray(OK)[None], vals, NEG_INF)
    return vals.reshape(H, 5, B_QROWS * GRID_W, B_KROWS * GRID_W).transpose(1, 0, 2, 3)


def _neighbourhood(proj, bias):
    B, S, _ = proj.shape
    sub = B_QROWS * GRID_W
    tq = min(NBR_BLOCKS_PER_STEP * sub, S)
    return pl.pallas_call(
        _nbr_kernel,
        grid=(B, S // tq),
        in_specs=[
            pl.BlockSpec((1, tq, BRANCH_W), lambda b, t: (b, t, CB_BQ)),
            pl.BlockSpec((1, S, BRANCH_W), lambda b, t: (b, 0, CB_BK)),
            pl.BlockSpec((1, S, BRANCH_W), lambda b, t: (b, 0, CB_BV)),
            pl.BlockSpec((5, 4, sub, B_KROWS * GRID_W), lambda b, t: (0, 0, 0, 0)),
        ],
        out_specs=pl.BlockSpec((1, tq, BRANCH_W), lambda b, t: (b, t, 0)),
        out_shape=jax.ShapeDtypeStruct((B, S, BRANCH_W), BF16),
        compiler_params=_params("arbitrary", "arbitrary"),
        name="neighbourhood",
    )(proj, proj, proj, bias)


def _merge_kernel(x_ref, gn_ref, oa_ref, ob_ref, od_ref, oc1_ref, oc2_ref, oc3_ref, l1_ref, l2_ref,
                  l3_ref, wg_ref, wb_ref, wo_ref, y_ref, il_ref):
    tm = x_ref.shape[1]
    x = x_ref[0]
    ms = jnp.mean(x * x, axis=-1, keepdims=True)
    h = ((x * lax.rsqrt(ms + EPS)) * gn_ref[...]).astype(BF16)

    def natural(ref):
        dil = ref.shape[0]
        if dil == 1:
            return ref[0].astype(F32)
        for r in range(dil):
            v = ref[r].astype(F32)
            for half in range(2):
                il_ref[half, pl.ds(r, tm // dil, stride=dil), :] = v[:, half * LANES:(half + 1) * LANES]
        return jnp.concatenate([il_ref[0], il_ref[1]], axis=1)

    l1, l2, l3 = natural(l1_ref), natural(l2_ref), natural(l3_ref)
    m = jnp.maximum(jnp.maximum(l1, l2), l3)
    e1, e2, e3 = jnp.exp(l1 - m), jnp.exp(l2 - m), jnp.exp(l3 - m)
    oc = (e1 * natural(oc1_ref) + e2 * natural(oc2_ref) + e3 * natural(oc3_ref)) / (e1 + e2 + e3)

    oc = oc.astype(BF16)
    rc = min(MERGE_ROW_CHUNK, tm)

    def gated_sum(c):
        rows = slice(c * rc, (c + 1) * rc)
        merged = None
        for i, o in enumerate((oa_ref[0, rows, :], ob_ref[0, rows, :], oc[rows, :], od_ref[0, rows, :])):
            gate = 0.5 + 0.5 * jnp.tanh(0.5 * jnp.dot(h[rows, :], wg_ref[i], preferred_element_type=F32))
            term = gate * jnp.dot(o, wb_ref[i], preferred_element_type=F32)
            merged = term if merged is None else merged + term
        return merged.astype(BF16)

    n_c = tm // rc
    pending = gated_sum(0)
    for c in range(n_c):
        ready = pending
        if c + 1 < n_c:
            pending = gated_sum(c + 1)
        rows = slice(c * rc, (c + 1) * rc)
        y_ref[0, rows, :] = x[rows, :] + jnp.dot(ready, wo_ref[...], preferred_element_type=F32)


def _merge(x, gain, o_a, o_b, o_d, oc, lse, w_gate, w_branch, w_out):
    B, S, _ = x.shape
    tm = min(TM_MERGE, S)
    tok256 = pl.BlockSpec((1, tm, BRANCH_W), lambda b, i: (b, i, 0))

    def grouped(arr):
        dil = arr.shape[1]
        return pl.BlockSpec((None, dil, tm // dil, BRANCH_W), lambda b, i: (b, 0, i, 0))

    def resident(shape):
        return pl.BlockSpec(shape, lambda b, i: (0,) * len(shape), pipeline_mode=pl.Buffered(1))

    return pl.pallas_call(
        _merge_kernel,
        grid=(B, S // tm),
        in_specs=[pl.BlockSpec((1, tm, D_MODEL), lambda b, i: (b, i, 0)),
                  resident((1, D_MODEL)),
                  tok256, tok256, tok256,
                  grouped(oc[0]), grouped(oc[1]), grouped(oc[2]),
                  grouped(lse[0]), grouped(lse[1]), grouped(lse[2]),
                  resident((N_BRANCH, D_MODEL, D_MODEL)),
                  resident((N_BRANCH, BRANCH_W, D_MODEL)),
                  resident((D_MODEL, D_MODEL))],
        out_specs=pl.BlockSpec((1, tm, D_MODEL), lambda b, i: (b, i, 0)),
        out_shape=jax.ShapeDtypeStruct((B, S, D_MODEL), F32),
        scratch_shapes=[pltpu.VMEM((BRANCH_W // LANES, tm, LANES), F32)],
        compiler_params=_params("arbitrary", "arbitrary"),
        name="merge_out",
    )(x, gain, o_a, o_b, o_d, oc[0], oc[1], oc[2], lse[0], lse[1], lse[2],
      w_gate, w_branch, w_out)


def _mlp_kernel(x_ref, xp_ref, xn_ref, g_ref, wu_ref, cw_ref, cb_ref, wd_ref, gf_ref, y_ref,
                hext_ref, u_ref, act_ref, *, final_norm):
    i = pl.program_id(1)
    tm = x_ref.shape[1]
    halo = MLP_HALO
    rc = min(MLP_ROW_CHUNK, tm)
    ext = rc + 2 * halo
    c = FF_CHUNK
    n_f = D_FF // c

    def norm(x, gain):
        ms = jnp.mean(x * x, axis=-1, keepdims=True)
        return (x * lax.rsqrt(ms + EPS)) * gain

    hext_ref[0:halo, :] = jnp.where(i > 0, norm(xp_ref[0], g_ref[...]), 0.0).astype(BF16)
    hext_ref[halo:halo + tm, :] = norm(x_ref[0], g_ref[...]).astype(BF16)
    hext_ref[halo + tm:, :] = jnp.where(i < pl.num_programs(1) - 1,
                                        norm(xn_ref[0], g_ref[...]), 0.0).astype(BF16)

    def up(r0, f):
        hx = hext_ref[pl.ds(r0, ext), :]
        for half in range(2):
            lo = half * D_FF + f * c
            u_ref[f % 2, half] = jnp.dot(hx, wu_ref[:, lo:lo + c], preferred_element_type=F32)

    def conv(u, lo):
        before = pltpu.roll(u, 1, 0)[halo:halo + rc, :]
        after = pltpu.roll(u, ext - 1, 0)[halo:halo + rc, :]
        return (before * cw_ref[0:1, lo:lo + c] + u[halo:halo + rc, :] * cw_ref[1:2, lo:lo + c]
                + after * cw_ref[2:3, lo:lo + c] + cb_ref[:, lo:lo + c])

    def row_chunk(ci, carry):
        r0 = pl.multiple_of(ci * rc, rc)
        up(r0, 0)
        for f in range(n_f):
            if f + 1 < n_f:
                up(r0, f + 1)
            val = conv(u_ref[f % 2, 0], f * c)
            gt = conv(u_ref[f % 2, 1], D_FF + f * c)
            act = (0.5 * gt * (1.0 + lax.erf(gt * math.sqrt(0.5)))) * val
            act_ref[:, f * c:(f + 1) * c] = act.astype(BF16)
        y = x_ref[0, pl.ds(r0, rc), :] + jnp.dot(act_ref[...], wd_ref[...],
                                                 preferred_element_type=F32)
        if final_norm:
            y = norm(y, gf_ref[...])
        y_ref[0, pl.ds(r0, rc), :] = y
        return carry

    lax.fori_loop(0, tm // rc, row_chunk, 0)


def _mlp(x, gain, w_up, conv_w, conv_b, w_down, gain_final, final_norm):
    B, S, _ = x.shape
    tm = min(TM_MLP, S)
    rc = min(MLP_ROW_CHUNK, tm)
    hb = tm // MLP_HALO
    n_i = S // tm
    kern = functools.partial(_mlp_kernel, final_norm=final_norm)

    def resident(shape):
        return pl.BlockSpec(shape, lambda b, i: (0,) * len(shape), pipeline_mode=pl.Buffered(1))

    return pl.pallas_call(
        kern,
        grid=(B, n_i),
        in_specs=[
            pl.BlockSpec((1, tm, D_MODEL), lambda b, i: (b, i, 0)),
            pl.BlockSpec((1, MLP_HALO, D_MODEL), lambda b, i: (b, jnp.maximum(i * hb - 1, 0), 0)),
            pl.BlockSpec((1, MLP_HALO, D_MODEL),
                         lambda b, i: (b, jnp.minimum((i + 1) * hb, S // MLP_HALO - 1), 0)),
            resident((1, D_MODEL)),
            resident((D_MODEL, 2 * D_FF)),
            resident((3, 2 * D_FF)),
            resident((1, 2 * D_FF)),
            resident((D_FF, D_MODEL)),
            resident((1, D_MODEL)),
        ],
        out_specs=pl.BlockSpec((1, tm, D_MODEL), lambda b, i: (b, i, 0)),
        out_shape=jax.ShapeDtypeStruct((B, S, D_MODEL), F32),
        scratch_shapes=[
            pltpu.VMEM((tm + 2 * MLP_HALO, D_MODEL), BF16),
            pltpu.VMEM((2, 2, rc + 2 * MLP_HALO, FF_CHUNK), F32),
            pltpu.VMEM((rc, D_FF), BF16),
        ],
        compiler_params=_params("arbitrary", "arbitrary"),
        name="mlp",
    )(x, x, x, gain, w_up, conv_w, conv_b, w_down, gain_final)


def _layer(x, l, tables, gmat, p):
    lam_init = 0.8 - 0.6 * math.exp(-0.3 * l)
    S = x.shape[1]
    w_in = p["w_in"][l][:, :N_MIX_COLS].astype(BF16)
    w_gate = p["w_in"][l][:, N_MIX_COLS:].reshape(D_MODEL, N_BRANCH, D_MODEL)
    w_gate = w_gate.transpose(1, 0, 2).astype(BF16)
    qk_gain = jnp.stack([jnp.tile(p["qk_norm"][l, 0], 4), jnp.tile(p["qk_norm"][l, 1], 4)]).astype(F32)
    proj, c_by4, c_by16 = _in_proj(x, p["norm_attn"][l][None, :], w_in, tables, gmat, qk_gain)

    lv = p["diff_lambda"][l].astype(F32)
    lam = (jnp.exp(jnp.sum(lv[0] * lv[1])) - jnp.exp(jnp.sum(lv[2] * lv[3])) + lam_init).reshape(1)
    tq = min(TQ_FULL, S)
    sub_gain = jnp.broadcast_to(p["diff_subln"][l].astype(F32)[:, None], (HEAD_DIM, tq))
    o_a = _full_attention(proj, lam, sub_gain, q_cb=CB_AQ, k_cb=CB_AK, v_cb=CB_AV, heads=_HEADS_A,
                          n_kv=4, v_row0=0, unroll=4, diff=True, post_scale=1.0 - lam_init)
    o_d = _full_attention(proj, lam, sub_gain, q_cb=CB_DQ, k_cb=CB_DKV, v_cb=CB_DKV, heads=_HEADS_D,
                          n_kv=2, v_row0=2 * HEAD_DIM, unroll=8, diff=False, post_scale=1.0)
    o_b = _neighbourhood(proj, _nbr_bias(p["na_rpb"][l], S))
    oc, lse = zip(_dilated_pattern(proj[:, None], 1, CB_CQ), _dilated_pattern(c_by4, 4, 0),
                  _dilated_pattern(c_by16, 16, 0))
    x = _merge(x, p["norm_attn"][l][None, :], o_a, o_b, o_d, oc, lse, w_gate,
               p["w_branch"][l].astype(BF16), p["w_out"][l].astype(BF16))
    return _mlp(x, p["norm_mlp"][l][None, :], p["w_up"][l].astype(BF16), p["conv_w"][l],
                p["conv_b"][l][None, :], p["w_down"][l].astype(BF16), p["norm_final"][None, :],
                final_norm=(l == DEPTH - 1))


def _trunk(x, tables, gmat, p):
    for l in range(DEPTH):
        x = _layer(x, l, tables, gmat, p)
    return x


def _group_mean_matrix():
    head = np.arange(BRANCH_W) // HEAD_DIM
    return jnp.asarray((head[:, None] == head[None, :]) / HEAD_DIM, dtype=BF16)


def kernel(x_prompt, x_sample, norm_attn, w_in, diff_lambda, diff_subln, na_rpb, qk_norm, w_branch,
           w_out, norm_mlp, w_up, conv_w, conv_b, w_down, norm_final):
    p = dict(norm_attn=norm_attn, w_in=w_in, diff_lambda=diff_lambda, diff_subln=diff_subln,
             na_rpb=na_rpb, qk_norm=qk_norm, w_branch=w_branch, w_out=w_out, norm_mlp=norm_mlp,
             w_up=w_up, conv_w=conv_w, conv_b=conv_b, w_down=w_down, norm_final=norm_final)
    gmat = _group_mean_matrix()
    outs = []
    for x in (x_prompt, x_sample):
        tables = _rope_tables(x.shape[1])
        outs.append(_trunk(x, tables, gmat, p))
    return tuple(outs)
```
